```python
import jax
import jax.numpy as jnp
from jax import lax
import numpy as np

D_MODEL = 2048
BATCH = 2
SEQ = 8192
DEPTH = 4

HEAD_DIM = 128
N_Q_HEADS = D_MODEL // HEAD_DIM
N_KV_HEADS = N_Q_HEADS // 4
Q_PER_KV = N_Q_HEADS // N_KV_HEADS
ATTN_WIDTH = N_Q_HEADS * HEAD_DIM
KV_WIDTH = N_KV_HEADS * HEAD_DIM
WINDOW = 128
BLOCK = 128
ROPE_THETA = 500000.0
ROT_DIM = HEAD_DIM // 4
CONV_WIDTH = D_MODEL // 2
CONV_SIZE = 31
CONV_PAD = CONV_SIZE // 2
POOL_WIDTH = D_MODEL // 2
POOL_SIZES = (2, 4, 8, 16)
N_POOL_GROUPS = len(POOL_SIZES)
POOL_GROUP = POOL_WIDTH // N_POOL_GROUPS
N_BRANCHES = 3
D_FF = ((8 * D_MODEL // 3 + 255) // 256) * 256
RMS_EPS = 1e-6
LN_EPS = 1e-5
NEG_INF = -1e30
IN_SIZES = (ATTN_WIDTH, KV_WIDTH, KV_WIDTH, CONV_WIDTH, CONV_WIDTH, POOL_WIDTH, N_BRANCHES * D_MODEL)
IN_SPLITS = [int(s) for s in np.cumsum(IN_SIZES)[:-1]]
N_IN = sum(IN_SIZES)

kernel_name = 'hybrid_gated_encoder_trunk'


def rmsnorm(x, g):
    xf = x.astype(jnp.float32)
    y = xf * lax.rsqrt(jnp.mean(xf * xf, axis=-1, keepdims=True) + RMS_EPS)
    return (y * g.astype(jnp.float32)).astype(x.dtype)


def layernorm(x, g, b):
    xf = x.astype(jnp.float32)
    mu = jnp.mean(xf, axis=-1, keepdims=True)
    var = jnp.mean(jnp.square(xf - mu), axis=-1, keepdims=True)
    y = (xf - mu) * lax.rsqrt(var + LN_EPS)
    return (y * g.astype(jnp.float32) + b.astype(jnp.float32)).astype(x.dtype)


def swiglu(x, w_gate, w_up, w_down):
    return (jax.nn.silu(x @ w_gate) * (x @ w_up)) @ w_down


def rope_tables(seq, dtype):
    pos = jnp.arange(seq, dtype=jnp.float32)
    inv_freq = ROPE_THETA ** (-jnp.arange(0, ROT_DIM, 2, dtype=jnp.float32) / ROT_DIM)
    ang = pos[:, None] * inv_freq[None, :]
    return jnp.cos(ang)[:, None, :].astype(dtype), jnp.sin(ang)[:, None, :].astype(dtype)


def partial_rotary(x, cos, sin):
    half = ROT_DIM // 2
    x1 = x[..., :half]
    x2 = x[..., half:ROT_DIM]
    return jnp.concatenate([x1 * cos - x2 * sin, x2 * cos + x1 * sin, x[..., ROT_DIM:]], axis=-1)


def windowed_gqa(q, k, v, sink):
    B, S = q.shape[0], q.shape[1]
    nb = S // BLOCK
    pad = ((0, 0), (BLOCK, BLOCK), (0, 0), (0, 0))
    kp = jnp.pad(k, pad).reshape(B, nb + 2, BLOCK, N_KV_HEADS, HEAD_DIM)
    vp = jnp.pad(v, pad).reshape(B, nb + 2, BLOCK, N_KV_HEADS, HEAD_DIM)
    kb = jnp.concatenate([kp[:, :nb], kp[:, 1:nb + 1], kp[:, 2:]], axis=2)
    vb = jnp.concatenate([vp[:, :nb], vp[:, 1:nb + 1], vp[:, 2:]], axis=2)
    qb = q.reshape(B, nb, BLOCK, N_KV_HEADS, Q_PER_KV, HEAD_DIM)
    logits = jnp.einsum('bnqhgd,bnkhd->bnhgqk', qb, kb).astype(jnp.float32) * (HEAD_DIM ** -0.5)
    qi = jnp.arange(BLOCK)[:, None]
    kj = jnp.arange(3 * BLOCK)[None, :]
    in_window = jnp.abs(kj - BLOCK - qi) <= WINDOW
    key_pos = jnp.arange(nb)[:, None] * BLOCK - BLOCK + jnp.arange(3 * BLOCK)[None, :]
    in_seq = (key_pos >= 0) & (key_pos < S)
    mask = in_window[None, :, :] & in_seq[:, None, :]
    logits = jnp.where(mask[None, :, None, None], logits, NEG_INF)
    s = sink.astype(jnp.float32).reshape(N_KV_HEADS, Q_PER_KV)[None, None, :, :, None, None]
    m = jnp.maximum(jnp.max(logits, axis=-1, keepdims=True), s)
    p = jnp.exp(logits - m)
    probs = p / (jnp.sum(p, axis=-1, keepdims=True) + jnp.exp(s - m))
    out = jnp.einsum('bnhgqk,bnkhd->bnqhgd', probs.astype(v.dtype), vb)
    return out.reshape(B, S, ATTN_WIDTH)


def conformer_conv(a, g, w_dw, b_dw, ln_g, ln_b, w_proj):
    y = a * jax.nn.sigmoid(g)
    y = lax.conv_general_dilated(y, w_dw[:, None, :], window_strides=(1,),
                                 padding=[(CONV_PAD, CONV_PAD)],
                                 dimension_numbers=('NWC', 'WIO', 'NWC'),
                                 feature_group_count=CONV_WIDTH) + b_dw
    y = jax.nn.silu(layernorm(y, ln_g, ln_b))
    return y @ w_proj


def multiscale_pool(p, pool_w, pool_scale, w_proj):
    B, S, _ = p.shape
    pg = p.reshape(B, S, N_POOL_GROUPS, POOL_GROUP).astype(jnp.float32)
    cs = jnp.pad(jnp.cumsum(pg, axis=1), ((0, 0), (1, 0), (0, 0), (0, 0)))
    pos = jnp.arange(S)[:, None]
    half = jnp.array(POOL_SIZES, dtype=jnp.int32)[None, :] // 2
    lo = jnp.clip(pos - half, 0, S - 1)
    hi = jnp.clip(pos + half - 1, 0, S - 1)
    grp = jnp.arange(N_POOL_GROUPS)[None, :]
    total = cs[:, hi + 1, grp, :] - cs[:, lo, grp, :]
    mean = total / (hi - lo + 1).astype(jnp.float32)[None, :, :, None]
    mixed = (mean - pg).astype(p.dtype)
    y = jnp.einsum('bsgc,gce->bsge', mixed, pool_w).reshape(B, S, POOL_WIDTH) * pool_scale
    return y @ w_proj


def hybrid_mixer(u, w_in, sink, w_attn_proj, conv_dw, conv_dw_b, conv_ln_g, conv_ln_b, w_conv_proj,
                 pool_w, pool_scale, w_pool_proj, w_out, cos, sin):
    B, S, _ = u.shape
    z = u @ w_in
    q, k, v, conv_a, conv_g, pool_in, gate_in = jnp.split(z, IN_SPLITS, axis=-1)
    q = partial_rotary(q.reshape(B, S, N_Q_HEADS, HEAD_DIM), cos, sin)
    k = partial_rotary(k.reshape(B, S, N_KV_HEADS, HEAD_DIM), cos, sin)
    v = v.reshape(B, S, N_KV_HEADS, HEAD_DIM)
    attn = windowed_gqa(q, k, v, sink) @ w_attn_proj
    conv = conformer_conv(conv_a, conv_g, conv_dw, conv_dw_b, conv_ln_g, conv_ln_b, w_conv_proj)
    pool = multiscale_pool(pool_in, pool_w, pool_scale, w_pool_proj)
    gates = jax.nn.sigmoid(gate_in.reshape(B, S, N_BRANCHES, D_MODEL))
    merged = gates[:, :, 0] * attn + gates[:, :, 1] * conv + gates[:, :, 2] * pool
    return merged @ w_out


def setup_inputs(seed: int = 0) -> dict:
    key = jax.random.key(seed)
    ks = iter(jax.random.split(key, 32))
    L, D = DEPTH, D_MODEL

    def dense(shape, fan_in):
        return jax.random.normal(next(ks), shape, jnp.float32) * (fan_in ** -0.5)

    def gain(shape):
        return 1.0 + 0.02 * jax.random.normal(next(ks), shape, jnp.float32)

    def small(shape, scale):
        return scale * jax.random.normal(next(ks), shape, jnp.float32)

    return {
        'x': jax.random.normal(next(ks), (BATCH, SEQ, D), jnp.float32),
        'ffn1_pre_g': gain((L, D)),
        'ffn1_post_g': gain((L, D)),
        'ffn1_w_gate': dense((L, D, D_FF), D),
        'ffn1_w_up': dense((L, D, D_FF), D),
        'ffn1_w_down': dense((L, D_FF, D), D_FF),
        'mix_pre_g': gain((L, D)),
        'mix_post_g': gain((L, D)),
        'w_in': dense((L, D, N_IN), D),
        'sink': small((L, N_Q_HEADS), 0.5),
        'w_attn_proj': dense((L, ATTN_WIDTH, D), ATTN_WIDTH),
        'conv_dw': dense((L, CONV_SIZE, CONV_WIDTH), CONV_SIZE),
        'conv_dw_b': small((L, CONV_WIDTH), 0.02),
        'conv_ln_g': gain((L, CONV_WIDTH)),
        'conv_ln_b': small((L, CONV_WIDTH), 0.02),
        'w_conv_proj': dense((L, CONV_WIDTH, D), CONV_WIDTH),
        'pool_w': dense((L, N_POOL_GROUPS, POOL_GROUP, POOL_GROUP), POOL_GROUP),
        'pool_scale': gain((L, POOL_WIDTH)),
        'w_pool_proj': dense((L, POOL_WIDTH, D), POOL_WIDTH),
        'w_out': dense((L, D, D), D),
        'ffn2_pre_g': gain((L, D)),
        'ffn2_post_g': gain((L, D)),
        'ffn2_w_gate': dense((L, D, D_FF), D),
        'ffn2_w_up': dense((L, D, D_FF), D),
        'ffn2_w_down': dense((L, D_FF, D), D_FF),
    }


def reference(x, ffn1_pre_g, ffn1_post_g, ffn1_w_gate, ffn1_w_up, ffn1_w_down,
              mix_pre_g, mix_post_g, w_in, sink, w_attn_proj, conv_dw, conv_dw_b, conv_ln_g, conv_ln_b,
              w_conv_proj, pool_w, pool_scale, w_pool_proj, w_out,
              ffn2_pre_g, ffn2_post_g, ffn2_w_gate, ffn2_w_up, ffn2_w_down):
    cos, sin = rope_tables(x.shape[1], x.dtype)
    h = x
    for l in range(DEPTH):
        f1 = swiglu(rmsnorm(h, ffn1_pre_g[l]), ffn1_w_gate[l], ffn1_w_up[l], ffn1_w_down[l])
        h = h + 0.5 * rmsnorm(f1, ffn1_post_g[l])
        u = rmsnorm(h, mix_pre_g[l])
        mix = hybrid_mixer(u, w_in[l], sink[l], w_attn_proj[l], conv_dw[l], conv_dw_b[l], conv_ln_g[l],
                           conv_ln_b[l], w_conv_proj[l], pool_w[l], pool_scale[l], w_pool_proj[l], w_out[l],
                           cos, sin)
        h = h + rmsnorm(mix, mix_post_g[l])
        f2 = swiglu(rmsnorm(h, ffn2_pre_g[l]), ffn2_w_gate[l], ffn2_w_up[l], ffn2_w_down[l])
        h = h + 0.5 * rmsnorm(f2, ffn2_post_g[l])
    return h
```

```python
import functools

import jax
import jax.numpy as jnp
from jax import lax
from jax.experimental import pallas as pl
from jax.experimental.pallas import tpu as pltpu

D_MODEL = 2048
DEPTH = 4
HEAD_DIM = 128
N_Q_HEADS = D_MODEL // HEAD_DIM
N_KV_HEADS = N_Q_HEADS // 4
Q_PER_KV = N_Q_HEADS // N_KV_HEADS
ATTN_WIDTH = N_Q_HEADS * HEAD_DIM
KV_WIDTH = N_KV_HEADS * HEAD_DIM
WINDOW = 128
BLOCK = 128
ROPE_THETA = 500000.0
ROT_DIM = HEAD_DIM // 4
ROT_HALF = ROT_DIM // 2
CONV_WIDTH = D_MODEL // 2
CONV_SIZE = 31
CONV_PAD = CONV_SIZE // 2
POOL_WIDTH = D_MODEL // 2
POOL_SIZES = (2, 4, 8, 16)
N_POOL_GROUPS = len(POOL_SIZES)
POOL_GROUP = POOL_WIDTH // N_POOL_GROUPS
N_BRANCHES = 3
D_FF = ((8 * D_MODEL // 3 + 255) // 256) * 256
RMS_EPS = 1e-6
LN_EPS = 1e-5
NEG_INF = -1e30

QKV_WIDTH = ATTN_WIDTH + 2 * KV_WIDTH
COL_CONV_A = QKV_WIDTH
COL_CONV_G = COL_CONV_A + CONV_WIDTH
COL_POOL = COL_CONV_G + CONV_WIDTH
COL_GATES = COL_POOL + POOL_WIDTH

LANES = 128
BF16_SUBLANES = 16
VMEM_LIMIT_BYTES = 56 * 1024 * 1024

NORM_ROWS = 512
FFN_TM = 1024
FFN_TF = 512
FFN_ROWS = 512
PROJ_TM = 1024
PROJ_TN = 512
ATT_QB = 1024
HALO = 16
CP_TS = 256
CP_ROWS = 32
CP_LANES = 256
MERGE_TM = 512
MERGE_TN = 256

F32 = jnp.float32
BF16 = jnp.bfloat16


def _params(*semantics):
    return pltpu.CompilerParams(dimension_semantics=semantics, vmem_limit_bytes=VMEM_LIMIT_BYTES)


def _rms_scale(x):
    return lax.rsqrt(jnp.sum(x * x, axis=-1, keepdims=True) * (1.0 / x.shape[-1]) + RMS_EPS)


def _rmsnorm_kernel(h_ref, g_ref, o_ref):
    x = h_ref[...]
    o_ref[...] = (x * _rms_scale(x) * g_ref[...]).astype(o_ref.dtype)


def _rmsnorm_bf16(h, g):
    t, d = h.shape
    return pl.pallas_call(
        _rmsnorm_kernel,
        grid=(t // NORM_ROWS,),
        in_specs=[pl.BlockSpec((NORM_ROWS, d), lambda i: (i, 0)),
                  pl.BlockSpec((1, d), lambda i: (0, 0))],
        out_specs=pl.BlockSpec((NORM_ROWS, d), lambda i: (i, 0)),
        out_shape=jax.ShapeDtypeStruct((t, d), BF16),
        compiler_params=_params("parallel"),
        name="rmsnorm",
    )(h, g.reshape(1, d))


def _ffn_kernel(h_ref, pre_g_ref, post_g_ref, wg_ref, wu_ref, wd_ref, o_ref, xn_ref):
    f = pl.program_id(1)
    n_f = pl.num_programs(1)
    tm = h_ref.shape[0]

    @pl.when(f == 0)
    def _():
        for r in range(0, tm, FFN_ROWS):
            x = h_ref[r:r + FFN_ROWS, :]
            xn_ref[r:r + FFN_ROWS, :] = (x * _rms_scale(x) * pre_g_ref[...]).astype(BF16)

    for r in range(0, tm, FFN_ROWS):
        xn = xn_ref[r:r + FFN_ROWS, :]
        g = jnp.dot(xn, wg_ref[...], preferred_element_type=F32)
        u = jnp.dot(xn, wu_ref[...], preferred_element_type=F32)
        a = (g * jax.nn.sigmoid(g) * u).astype(BF16)
        d = jnp.dot(a, wd_ref[...], preferred_element_type=F32)

        @pl.when(f == 0)
        def _():
            o_ref[r:r + FFN_ROWS, :] = d

        @pl.when(f > 0)
        def _():
            o_ref[r:r + FFN_ROWS, :] += d

    @pl.when(f == n_f - 1)
    def _():
        for r in range(0, tm, FFN_ROWS):
            y = o_ref[r:r + FFN_ROWS, :]
            o_ref[r:r + FFN_ROWS, :] = h_ref[r:r + FFN_ROWS, :] + 0.5 * (y * _rms_scale(y) * post_g_ref[...])


def _ffn_block(h, pre_g, post_g, wg, wu, wd):
    t, d = h.shape
    ff = wg.shape[1]
    return pl.pallas_call(
        _ffn_kernel,
        grid=(t // FFN_TM, ff // FFN_TF),
        in_specs=[pl.BlockSpec((FFN_TM, d), lambda i, f: (i, 0), pipeline_mode=pl.Buffered(1)),
                  pl.BlockSpec((1, d), lambda i, f: (0, 0)),
                  pl.BlockSpec((1, d), lambda i, f: (0, 0)),
                  pl.BlockSpec((d, FFN_TF), lambda i, f: (0, f)),
                  pl.BlockSpec((d, FFN_TF), lambda i, f: (0, f)),
                  pl.BlockSpec((FFN_TF, d), lambda i, f: (f, 0))],
        out_specs=pl.BlockSpec((FFN_TM, d), lambda i, f: (i, 0)),
        out_shape=jax.ShapeDtypeStruct((t, d), F32),
        scratch_shapes=[pltpu.VMEM((FFN_TM, d), BF16)],
        compiler_params=_params("parallel", "arbitrary"),
        name="ffn",
    )(h, pre_g.reshape(1, d), post_g.reshape(1, d), wg, wu, wd)


def _qkv_kernel(u_ref, w_ref, cos_ref, nsin_ref, psin_ref, o_ref):
    j = pl.program_id(1)
    z = jnp.dot(u_ref[...], w_ref[...], preferred_element_type=F32)
    n_rot_steps = (ATTN_WIDTH + KV_WIDTH) // PROJ_TN

    @pl.when(j < n_rot_steps)
    def _():
        c = cos_ref[...]
        s1 = nsin_ref[...]
        s2 = psin_ref[...]
        for a in range(PROJ_TN // HEAD_DIM):
            x = z[:, a * HEAD_DIM:(a + 1) * HEAD_DIM]
            up = pltpu.roll(x, HEAD_DIM - ROT_HALF, axis=1)
            dn = pltpu.roll(x, ROT_HALF, axis=1)
            o_ref[:, a * HEAD_DIM:(a + 1) * HEAD_DIM] = (x * c + up * s1 + dn * s2).astype(o_ref.dtype)

    @pl.when(j >= n_rot_steps)
    def _():
        o_ref[...] = z.astype(o_ref.dtype)


def _qkv_proj(u, w_in, cos_t, nsin_t, psin_t, seq):
    t, d = u.shape
    s_tiles = seq // PROJ_TM
    tab = pl.BlockSpec((PROJ_TM, HEAD_DIM), lambda i, j: (i % s_tiles, 0))
    return pl.pallas_call(
        _qkv_kernel,
        grid=(t // PROJ_TM, QKV_WIDTH // PROJ_TN),
        in_specs=[pl.BlockSpec((PROJ_TM, d), lambda i, j: (i, 0)),
                  pl.BlockSpec((d, PROJ_TN), lambda i, j: (0, j)),
                  tab, tab, tab],
        out_specs=pl.BlockSpec((PROJ_TM, PROJ_TN), lambda i, j: (i, j)),
        out_shape=jax.ShapeDtypeStruct((t, QKV_WIDTH), BF16),
        compiler_params=_params("parallel", "arbitrary"),
        name="qkv_proj",
    )(u, w_in, cos_t, nsin_t, psin_t)


def _glu_pool_kernel(u_ref, wa_ref, wg_ref, wp_ref, y_ref, p_ref):
    u = u_ref[...]
    a = jnp.dot(u, wa_ref[...], preferred_element_type=F32)
    g = jnp.dot(u, wg_ref[...], preferred_element_type=F32)
    y_ref[...] = (a * jax.nn.sigmoid(g)).astype(y_ref.dtype)
    p_ref[...] = jnp.dot(u, wp_ref[...], preferred_element_type=F32).astype(p_ref.dtype)


def _glu_pool_proj(u, w_in):
    t, d = u.shape
    col = lambda start: (lambda i, j: (0, start // PROJ_TN + j))
    out = pl.BlockSpec((PROJ_TM, PROJ_TN), lambda i, j: (i, j))
    return pl.pallas_call(
        _glu_pool_kernel,
        grid=(t // PROJ_TM, CONV_WIDTH // PROJ_TN),
        in_specs=[pl.BlockSpec((PROJ_TM, d), lambda i, j: (i, 0)),
                  pl.BlockSpec((d, PROJ_TN), col(COL_CONV_A)),
                  pl.BlockSpec((d, PROJ_TN), col(COL_CONV_G)),
                  pl.BlockSpec((d, PROJ_TN), col(COL_POOL))],
        out_specs=[out, out],
        out_shape=[jax.ShapeDtypeStruct((t, CONV_WIDTH), BF16),
                   jax.ShapeDtypeStruct((t, POOL_WIDTH), BF16)],
        compiler_params=_params("parallel", "arbitrary"),
        name="glu_pool_proj",
    )(u, w_in, w_in, w_in)


def _attn_kernel(sink_ref, q_ref, kp_ref, km_ref, kn_ref, vp_ref, vm_ref, vn_ref, o_ref,
                 kx_ref, vx_ref, wb_ref):
    i = pl.program_id(1)
    n_sub = ATT_QB // BLOCK
    seq = pl.num_programs(1) * ATT_QB
    keys = 3 * BLOCK

    kx_ref[0:BLOCK, :] = kp_ref[...]
    kx_ref[BLOCK:BLOCK + ATT_QB, :] = km_ref[...]
    kx_ref[BLOCK + ATT_QB:, :] = kn_ref[...]
    vx_ref[0:BLOCK, :] = vp_ref[...]
    vx_ref[BLOCK:BLOCK + ATT_QB, :] = vm_ref[...]
    vx_ref[BLOCK + ATT_QB:, :] = vn_ref[...]

    qi = lax.broadcasted_iota(jnp.int32, (BLOCK, keys), 0)
    kj = lax.broadcasted_iota(jnp.int32, (BLOCK, keys), 1)
    wb_ref[...] = jnp.where(jnp.abs(kj - BLOCK - qi) <= WINDOW, 0.0, NEG_INF).astype(F32)

    def sub_block(j, carry):
        r0 = pl.multiple_of(j * BLOCK, BLOCK)
        key_pos = (i * n_sub + j) * BLOCK - BLOCK + lax.broadcasted_iota(jnp.int32, (1, keys), 1)
        bias = wb_ref[...] + jnp.where((key_pos >= 0) & (key_pos < seq), 0.0, NEG_INF).astype(F32)
        for hh in range(N_KV_HEADS):
            kc = kx_ref[pl.ds(r0, keys), hh * HEAD_DIM:(hh + 1) * HEAD_DIM]
            vc = vx_ref[pl.ds(r0, keys), hh * HEAD_DIM:(hh + 1) * HEAD_DIM]
            qs = jnp.concatenate(
                [q_ref[pl.ds(r0, BLOCK), (hh * Q_PER_KV + g) * HEAD_DIM:(hh * Q_PER_KV + g + 1) * HEAD_DIM]
                 for g in range(Q_PER_KV)], axis=0)
            s = lax.dot_general(qs, kc, (((1,), (1,)), ((), ())), preferred_element_type=F32)
            probs = []
            for g in range(Q_PER_KV):
                sk = sink_ref[hh * Q_PER_KV + g]
                lg = s[g * BLOCK:(g + 1) * BLOCK, :] * (HEAD_DIM ** -0.5) + bias
                m = jnp.maximum(jnp.max(lg, axis=-1, keepdims=True), sk)
                p = jnp.exp(lg - m)
                den = jnp.sum(p, axis=-1, keepdims=True) + jnp.exp(sk - m)
                probs.append((p * (1.0 / den)).astype(BF16))
            pv = jnp.dot(jnp.concatenate(probs, axis=0), vc, preferred_element_type=F32)
            for g in range(Q_PER_KV):
                c0 = (hh * Q_PER_KV + g) * HEAD_DIM
                o_ref[pl.ds(r0, BLOCK), c0:c0 + HEAD_DIM] = pv[g * BLOCK:(g + 1) * BLOCK, :].astype(o_ref.dtype)
        return carry

    lax.fori_loop(0, n_sub, sub_block, 0)


def _windowed_attention(qkv, sink, batch, seq):
    qkv3 = qkv.reshape(batch, seq, QKV_WIDTH)
    n_sub = ATT_QB // BLOCK
    n_blk = seq // BLOCK
    k_col = ATTN_WIDTH // KV_WIDTH
    v_col = k_col + 1

    def halo(col):
        prev = pl.BlockSpec((None, BLOCK, KV_WIDTH), lambda b, i: (b, jnp.maximum(i * n_sub - 1, 0), col))
        main = pl.BlockSpec((None, ATT_QB, KV_WIDTH), lambda b, i: (b, i, col))
        nxt = pl.BlockSpec((None, BLOCK, KV_WIDTH), lambda b, i: (b, jnp.minimum((i + 1) * n_sub, n_blk - 1), col))
        return [prev, main, nxt]

    out = pl.pallas_call(
        _attn_kernel,
        grid=(batch, seq // ATT_QB),
        in_specs=[pl.BlockSpec(memory_space=pltpu.SMEM),
                  pl.BlockSpec((None, ATT_QB, ATTN_WIDTH), lambda b, i: (b, i, 0))] + halo(k_col) + halo(v_col),
        out_specs=pl.BlockSpec((None, ATT_QB, ATTN_WIDTH), lambda b, i: (b, i, 0)),
        out_shape=jax.ShapeDtypeStruct((batch, seq, ATTN_WIDTH), BF16),
        scratch_shapes=[pltpu.VMEM((ATT_QB + 2 * BLOCK, KV_WIDTH), BF16),
                        pltpu.VMEM((ATT_QB + 2 * BLOCK, KV_WIDTH), BF16),
                        pltpu.VMEM((BLOCK, 3 * BLOCK), F32)],
        compiler_params=_params("parallel", "arbitrary"),
        name="window_attn",
    )(sink, qkv3, qkv3, qkv3, qkv3, qkv3, qkv3, qkv3)
    return out.reshape(batch * seq, ATTN_WIDTH)


def _conv_pool_kernel(yp_ref, ym_ref, yn_ref, pp_ref, pm_ref, pn_ref, wdw_ref, bdw_ref, lng_ref, lnb_ref,
                      pw_ref, ps_ref, co_ref, po_ref, yx_ref, px_ref, cb_ref):
    i = pl.program_id(1)
    n_i = pl.num_programs(1)
    ts = ym_ref.shape[0]
    seq = n_i * ts

    has_prev = (i > 0).astype(F32)
    has_next = (i < n_i - 1).astype(F32)
    yx_ref[0:HALO, :] = yp_ref[...].astype(F32) * has_prev
    yx_ref[HALO:HALO + ts, :] = ym_ref[...].astype(F32)
    yx_ref[HALO + ts:, :] = yn_ref[...].astype(F32) * has_next
    px_ref[0:HALO, :] = pp_ref[...].astype(F32) * has_prev
    px_ref[HALO:HALO + ts, :] = pm_ref[...].astype(F32)
    px_ref[HALO + ts:, :] = pn_ref[...].astype(F32) * has_next

    for c in range(0, CONV_WIDTH, CP_LANES):
        for r in range(0, ts, CP_ROWS):
            acc = jnp.zeros((CP_ROWS, CP_LANES), F32)
            for k in range(CONV_SIZE):
                r_src = HALO + r + k - CONV_PAD
                acc = acc + yx_ref[r_src:r_src + CP_ROWS, c:c + CP_LANES] * wdw_ref[k:k + 1, c:c + CP_LANES]
            cb_ref[r:r + CP_ROWS, c:c + CP_LANES] = acc + bdw_ref[:, c:c + CP_LANES]

    x = cb_ref[...]
    mu = jnp.sum(x, axis=-1, keepdims=True) * (1.0 / CONV_WIDTH)
    xc = x - mu
    var = jnp.sum(xc * xc, axis=-1, keepdims=True) * (1.0 / CONV_WIDTH)
    yn = xc * lax.rsqrt(var + LN_EPS) * lng_ref[...] + lnb_ref[...]
    co_ref[...] = (yn * jax.nn.sigmoid(yn)).astype(co_ref.dtype)

    pos = i * ts + lax.broadcasted_iota(jnp.int32, (ts, 1), 0)
    for gi, size in enumerate(POOL_SIZES):
        half = size // 2
        c0 = gi * POOL_GROUP
        tot = jnp.zeros((ts, POOL_GROUP), F32)
        for dlt in range(-half, half):
            tot = tot + px_ref[HALO + dlt:HALO + dlt + ts, c0:c0 + POOL_GROUP]
        lo = jnp.clip(pos - half, 0, seq - 1)
        hi = jnp.clip(pos + half - 1, 0, seq - 1)
        cnt = (hi - lo + 1).astype(F32)
        mixed = (tot / cnt - px_ref[HALO:HALO + ts, c0:c0 + POOL_GROUP]).astype(BF16)
        yg = jnp.dot(mixed, pw_ref[gi], preferred_element_type=F32)
        po_ref[:, c0:c0 + POOL_GROUP] = (yg * ps_ref[:, c0:c0 + POOL_GROUP]).astype(po_ref.dtype)


def _conv_pool(y, p, w_dw, b_dw, ln_g, ln_b, pool_w, pool_scale, batch, seq):
    y3 = y.reshape(batch, seq, CONV_WIDTH)
    p3 = p.reshape(batch, seq, POOL_WIDTH)
    per = CP_TS // HALO
    n_halo = seq // HALO

    def halo(width):
        prev = pl.BlockSpec((None, HALO, width), lambda b, i: (b, jnp.maximum(i * per - 1, 0), 0))
        main = pl.BlockSpec((None, CP_TS, width), lambda b, i: (b, i, 0))
        nxt = pl.BlockSpec((None, HALO, width), lambda b, i: (b, jnp.minimum((i + 1) * per, n_halo - 1), 0))
        return [prev, main, nxt]

    full = lambda shape: pl.BlockSpec(shape, lambda b, i: (0,) * len(shape))
    conv, pool = pl.pallas_call(
        _conv_pool_kernel,
        grid=(batch, seq // CP_TS),
        in_specs=halo(CONV_WIDTH) + halo(POOL_WIDTH) + [
            full((CONV_SIZE, CONV_WIDTH)), full((1, CONV_WIDTH)), full((1, CONV_WIDTH)), full((1, CONV_WIDTH)),
            full((N_POOL_GROUPS, POOL_GROUP, POOL_GROUP)), full((1, POOL_WIDTH))],
        out_specs=[pl.BlockSpec((None, CP_TS, CONV_WIDTH), lambda b, i: (b, i, 0)),
                   pl.BlockSpec((None, CP_TS, POOL_WIDTH), lambda b, i: (b, i, 0))],
        out_shape=[jax.ShapeDtypeStruct((batch, seq, CONV_WIDTH), BF16),
                   jax.ShapeDtypeStruct((batch, seq, POOL_WIDTH), BF16)],
        scratch_shapes=[pltpu.VMEM((CP_TS + 2 * HALO, CONV_WIDTH), F32),
                        pltpu.VMEM((CP_TS + 2 * HALO, POOL_WIDTH), F32),
                        pltpu.VMEM((CP_TS, CONV_WIDTH), F32)],
        compiler_params=_params("parallel", "arbitrary"),
        name="conv_pool",
    )(y3, y3, y3, p3, p3, p3, w_dw, b_dw.reshape(1, -1), ln_g.reshape(1, -1), ln_b.reshape(1, -1),
      pool_w, pool_scale.reshape(1, -1))
    return conv.reshape(batch * seq, CONV_WIDTH), pool.reshape(batch * seq, POOL_WIDTH)


def _merge_kernel(h_ref, u_ref, at_ref, cv_ref, pl_ref, wg0_ref, wg1_ref, wg2_ref, wa_ref, wc_ref, wp_ref,
                  wo_ref, post_g_ref, o_ref):
    n = pl.program_id(1)
    n_n = pl.num_programs(1)
    u = u_ref[...]

    def gate(w_ref):
        return jax.nn.sigmoid(jnp.dot(u, w_ref[...], preferred_element_type=F32))

    m = gate(wg0_ref) * jnp.dot(at_ref[...], wa_ref[...], preferred_element_type=F32)
    m = m + gate(wg1_ref) * jnp.dot(cv_ref[...], wc_ref[...], preferred_element_type=F32)
    m = m + gate(wg2_ref) * jnp.dot(pl_ref[...], wp_ref[...], preferred_element_type=F32)
    d = jnp.dot(m.astype(BF16), wo_ref[...], preferred_element_type=F32)

    @pl.when(n == 0)
    def _():
        o_ref[...] = d

    @pl.when(n > 0)
    def _():
        o_ref[...] += d

    @pl.when(n == n_n - 1)
    def _():
        y = o_ref[...]
        o_ref[...] = h_ref[...] + y * _rms_scale(y) * post_g_ref[...]


def _merge_block(h, u, attn, conv, pool, w_in, w_attn, w_conv, w_pool, w_out, post_g):
    t, d = h.shape
    row = lambda width: pl.BlockSpec((MERGE_TM, width), lambda i, n: (i, 0))
    gate_col = lambda b: pl.BlockSpec((d, MERGE_TN), lambda i, n: (0, (COL_GATES + b * d) // MERGE_TN + n))
    col = lambda k: pl.BlockSpec((k, MERGE_TN), lambda i, n: (0, n))
    return pl.pallas_call(
        _merge_kernel,
        grid=(t // MERGE_TM, d // MERGE_TN),
        in_specs=[row(d), row(d), row(ATTN_WIDTH), row(CONV_WIDTH), row(POOL_WIDTH),
                  gate_col(0), gate_col(1), gate_col(2),
                  col(ATTN_WIDTH), col(CONV_WIDTH), col(POOL_WIDTH),
                  pl.BlockSpec((MERGE_TN, d), lambda i, n: (n, 0)),
                  pl.BlockSpec((1, d), lambda i, n: (0, 0))],
        out_specs=row(d),
        out_shape=jax.ShapeDtypeStruct((t, d), F32),
        compiler_params=_params("parallel", "arbitrary"),
        name="merge",
    )(h, u, attn, conv, pool, w_in, w_in, w_in, w_attn, w_conv, w_pool, w_out, post_g.reshape(1, d))


def _rope_lane_tables(seq):
    pos = jnp.arange(seq, dtype=F32)
    inv_freq = ROPE_THETA ** (-jnp.arange(0, ROT_DIM, 2, dtype=F32) / ROT_DIM)
    ang = pos[:, None] * inv_freq[None, :]
    cos, sin = jnp.cos(ang), jnp.sin(ang)
    ones = jnp.ones((seq, HEAD_DIM - ROT_DIM), F32)
    zeros_half = jnp.zeros((seq, ROT_HALF), F32)
    zeros_rest = jnp.zeros((seq, HEAD_DIM - ROT_DIM), F32)
    cos_t = jnp.concatenate([cos, cos, ones], axis=-1)
    nsin_t = jnp.concatenate([-sin, zeros_half, zeros_rest], axis=-1)
    psin_t = jnp.concatenate([zeros_half, sin, zeros_rest], axis=-1)
    return cos_t, nsin_t, psin_t


def kernel(x, ffn1_pre_g, ffn1_post_g, ffn1_w_gate, ffn1_w_up, ffn1_w_down, mix_pre_g, mix_post_g, w_in, sink,
           w_attn_proj, conv_dw, conv_dw_b, conv_ln_g, conv_ln_b, w_conv_proj, pool_w, pool_scale, w_pool_proj,
           w_out, ffn2_pre_g, ffn2_post_g, ffn2_w_gate, ffn2_w_up, ffn2_w_down):
    batch, seq, d = x.shape
    cos_t, nsin_t, psin_t = _rope_lane_tables(seq)
    h = x.reshape(batch * seq, d)
    bf = lambda w: w.astype(BF16)
    for l in range(DEPTH):
        h = _ffn_block(h, ffn1_pre_g[l], ffn1_post_g[l], bf(ffn1_w_gate[l]), bf(ffn1_w_up[l]), bf(ffn1_w_down[l]))
        u = _rmsnorm_bf16(h, mix_pre_g[l])
        w_in_l = bf(w_in[l])
        qkv = _qkv_proj(u, w_in_l, cos_t, nsin_t, psin_t, seq)
        y, p = _glu_pool_proj(u, w_in_l)
        attn = _windowed_attention(qkv, sink[l], batch, seq)
        conv, pool = _conv_pool(y, p, conv_dw[l], conv_dw_b[l], conv_ln_g[l], conv_ln_b[l],
                                bf(pool_w[l]), pool_scale[l], batch, seq)
        h = _merge_block(h, u, attn, conv, pool, w_in_l, bf(w_attn_proj[l]), bf(w_conv_proj[l]),
                         bf(w_pool_proj[l]), bf(w_out[l]), mix_post_g[l])
        h = _ffn_block(h, ffn2_pre_g[l], ffn2_post_g[l], bf(ffn2_w_gate[l]), bf(ffn2_w_up[l]), bf(ffn2_w_down[l]))
    return h.reshape(batch, seq, d)
```

```python
import functools
import math

import jax
import jax.numpy as jnp
from jax import lax
from jax.experimental import pallas as pl
from jax.experimental.pallas import tpu as pltpu

D_MODEL = 2048
DEPTH = 4
HEAD_DIM = 128
N_Q_HEADS = D_MODEL // HEAD_DIM
N_KV_HEADS = N_Q_HEADS // 4
Q_PER_KV = N_Q_HEADS // N_KV_HEADS
ATTN_WIDTH = N_Q_HEADS * HEAD_DIM
KV_WIDTH = N_KV_HEADS * HEAD_DIM
WINDOW = 128
BLOCK = 128
ROPE_THETA = 500000.0
ROT_DIM = HEAD_DIM // 4
ROT_HALF = ROT_DIM // 2
CONV_WIDTH = D_MODEL // 2
CONV_SIZE = 31
CONV_PAD = CONV_SIZE // 2
POOL_WIDTH = D_MODEL // 2
POOL_SIZES = (2, 4, 8, 16)
N_POOL_GROUPS = len(POOL_SIZES)
POOL_GROUP = POOL_WIDTH // N_POOL_GROUPS
N_BRANCHES = 3
D_FF = ((8 * D_MODEL // 3 + 255) // 256) * 256
RMS_EPS = 1e-6
LN_EPS = 1e-5
NEG_INF = -1e30
LOG2_E = math.log2(math.e)

QKV_WIDTH = ATTN_WIDTH + 2 * KV_WIDTH
COL_CONV_A = QKV_WIDTH
COL_CONV_G = COL_CONV_A + CONV_WIDTH
COL_POOL = COL_CONV_G + CONV_WIDTH
COL_GATES = COL_POOL + POOL_WIDTH

LANES = 128
SUBLANES = 8
BF16_SUBLANES = 16
MXU_COLS = 256
VMEM_LIMIT_BYTES = 56 * 1024 * 1024

NORM_ROWS = 512
UP_TM = 1024
UP_TF = 512
DOWN_TM = 512
OUT_TM = 512
RES_TN = 512
RES_ROWS = 128
PROJ_TM = 1024
PROJ_TN = 512
ATT_QB = 1024
HALO = 16
CP_TS = 256
CP_ROWS = 64
MERGE_TM = 1024
MERGE_TN = 256

F32 = jnp.float32
BF16 = jnp.bfloat16


def _params(*semantics):
    return pltpu.CompilerParams(dimension_semantics=semantics, vmem_limit_bytes=VMEM_LIMIT_BYTES)


def _rms_scale(x):
    return lax.rsqrt(jnp.sum(x * x, axis=-1, keepdims=True) * (1.0 / x.shape[-1]) + RMS_EPS)


def _layer_cols(l, k, tn, col0=0):
    return pl.BlockSpec((None, k, tn), lambda i, j: (l, 0, col0 // tn + j))


def _rmsnorm_kernel(h_ref, g_ref, o_ref):
    x = h_ref[...]
    o_ref[...] = (x * _rms_scale(x) * g_ref[...]).astype(o_ref.dtype)


def _rmsnorm_bf16(h, g):
    t, d = h.shape
    return pl.pallas_call(
        _rmsnorm_kernel,
        grid=(t // NORM_ROWS,),
        in_specs=[pl.BlockSpec((NORM_ROWS, d), lambda i: (i, 0)),
                  pl.BlockSpec((1, d), lambda i: (0, 0))],
        out_specs=pl.BlockSpec((NORM_ROWS, d), lambda i: (i, 0)),
        out_shape=jax.ShapeDtypeStruct((t, d), BF16),
        compiler_params=_params("parallel"),
        name="rmsnorm",
    )(h, g.reshape(1, d))


def _gate_up_kernel(x_ref, wg_ref, wu_ref, o_ref):
    x = x_ref[...]
    g = jnp.dot(x, wg_ref[...], preferred_element_type=F32)
    u = jnp.dot(x, wu_ref[...], preferred_element_type=F32)
    o_ref[...] = (g * jax.nn.sigmoid(g) * u).astype(o_ref.dtype)


def _gate_up(xn, w_gate, w_up, l):
    t, d = xn.shape
    ff = w_gate.shape[2]
    return pl.pallas_call(
        _gate_up_kernel,
        grid=(t // UP_TM, ff // UP_TF),
        in_specs=[pl.BlockSpec((UP_TM, d), lambda i, j: (i, 0)),
                  _layer_cols(l, d, UP_TF), _layer_cols(l, d, UP_TF)],
        out_specs=pl.BlockSpec((UP_TM, UP_TF), lambda i, j: (i, j)),
        out_shape=jax.ShapeDtypeStruct((t, ff), BF16),
        compiler_params=_params("parallel", "arbitrary"),
        name="gate_up",
    )(xn, w_gate, w_up)


def _proj_residual_kernel(a_ref, w_ref, h_ref, post_g_ref, *rest, res_scale, emit_next):
    if emit_next:
        next_g_ref, o_ref, nx_ref = rest
    else:
        o_ref, = rest
    j = pl.program_id(1)
    tm = o_ref.shape[0]
    tn = w_ref.shape[1]
    col = pl.multiple_of(j * tn, tn)
    o_ref[:, pl.ds(col, tn)] = jnp.dot(a_ref[...], w_ref[...], preferred_element_type=F32)

    @pl.when(j == pl.num_programs(1) - 1)
    def _():
        for r in range(0, tm, RES_ROWS):
            y = o_ref[r:r + RES_ROWS, :]
            hn = h_ref[r:r + RES_ROWS, :] + res_scale * (y * _rms_scale(y) * post_g_ref[...])
            o_ref[r:r + RES_ROWS, :] = hn
            if emit_next:
                nx_ref[r:r + RES_ROWS, :] = (hn * _rms_scale(hn) * next_g_ref[...]).astype(nx_ref.dtype)


def _proj_residual(a, w, l, h, post_g, next_g, res_scale, tm):
    t, d = h.shape
    k = a.shape[1]
    emit_next = next_g is not None
    vec = pl.BlockSpec((1, d), lambda i, j: (0, 0))
    row = pl.BlockSpec((tm, d), lambda i, j: (i, 0))
    in_specs = [pl.BlockSpec((tm, k), lambda i, j: (i, 0)), _layer_cols(l, k, RES_TN), row, vec]
    args = [a, w, h, post_g.reshape(1, d)]
    out_specs, out_shape = [row], [jax.ShapeDtypeStruct((t, d), F32)]
    if emit_next:
        in_specs.append(vec)
        args.append(next_g.reshape(1, d))
        out_specs.append(row)
        out_shape.append(jax.ShapeDtypeStruct((t, d), BF16))
    outs = pl.pallas_call(
        functools.partial(_proj_residual_kernel, res_scale=res_scale, emit_next=emit_next),
        grid=(t // tm, d // RES_TN),
        in_specs=in_specs, out_specs=out_specs, out_shape=out_shape,
        compiler_params=_params("parallel", "arbitrary"),
        name="proj_residual",
    )(*args)
    return (outs[0], outs[1]) if emit_next else (outs[0], None)


def _qkv_kernel(u_ref, w_ref, cos_ref, sin_ref, o_ref):
    j = pl.program_id(1)
    is_rot = j < (ATTN_WIDTH + KV_WIDTH) // PROJ_TN
    c = jnp.where(is_rot, cos_ref[...], 1.0)
    s = jnp.where(is_rot, sin_ref[...], 0.0)
    u = u_ref[...]
    for n in range(0, PROJ_TN, MXU_COLS):
        z = jnp.dot(u, w_ref[:, n:n + MXU_COLS], preferred_element_type=F32)
        for a in range(0, MXU_COLS, HEAD_DIM):
            x = z[:, a:a + HEAD_DIM]
            partner = pltpu.roll(x, HEAD_DIM // 2, axis=1)
            o_ref[:, n + a:n + a + HEAD_DIM] = (x * c + partner * s).astype(o_ref.dtype)


def _qkv_proj(u, w_qkv, l, cos_t, sin_t, seq):
    t, d = u.shape
    s_tiles = seq // PROJ_TM
    tab = pl.BlockSpec((PROJ_TM, HEAD_DIM), lambda i, j: (i % s_tiles, 0))
    return pl.pallas_call(
        _qkv_kernel,
        grid=(t // PROJ_TM, QKV_WIDTH // PROJ_TN),
        in_specs=[pl.BlockSpec((PROJ_TM, d), lambda i, j: (i, 0)),
                  _layer_cols(l, d, PROJ_TN), tab, tab],
        out_specs=pl.BlockSpec((PROJ_TM, PROJ_TN), lambda i, j: (i, j)),
        out_shape=jax.ShapeDtypeStruct((t, QKV_WIDTH), BF16),
        compiler_params=_params("parallel", "arbitrary"),
        name="qkv_proj",
    )(u, w_qkv, cos_t, sin_t)


def _glu_pool_kernel(u_ref, wa_ref, wg_ref, wp_ref, y_ref, p_ref):
    u = u_ref[...]
    a = jnp.dot(u, wa_ref[...], preferred_element_type=F32)
    g = jnp.dot(u, wg_ref[...], preferred_element_type=F32)
    y_ref[...] = (a * jax.nn.sigmoid(g)).astype(y_ref.dtype)
    p_ref[...] = jnp.dot(u, wp_ref[...], preferred_element_type=F32).astype(p_ref.dtype)


def _glu_pool_proj(u, w_in, l):
    t, d = u.shape
    out = pl.BlockSpec((PROJ_TM, PROJ_TN), lambda i, j: (i, j))
    return pl.pallas_call(
        _glu_pool_kernel,
        grid=(t // PROJ_TM, CONV_WIDTH // PROJ_TN),
        in_specs=[pl.BlockSpec((PROJ_TM, d), lambda i, j: (i, 0)),
                  _layer_cols(l, d, PROJ_TN, COL_CONV_A),
                  _layer_cols(l, d, PROJ_TN, COL_CONV_G),
                  _layer_cols(l, d, PROJ_TN, COL_POOL)],
        out_specs=[out, out],
        out_shape=[jax.ShapeDtypeStruct((t, CONV_WIDTH), BF16),
                   jax.ShapeDtypeStruct((t, POOL_WIDTH), BF16)],
        compiler_params=_params("parallel", "arbitrary"),
        name="glu_pool_proj",
    )(u, w_in, w_in, w_in)


def _attn_kernel(sink_ref, q_ref, kp_ref, km_ref, kn_ref, vp_ref, vm_ref, vn_ref, o_ref,
                 kx_ref, vx_ref, wb_ref):
    i = pl.program_id(1)
    n_sub = ATT_QB // BLOCK
    seq = pl.num_programs(1) * ATT_QB
    keys = 3 * BLOCK

    kx_ref[0:BLOCK, :] = kp_ref[...]
    kx_ref[BLOCK:BLOCK + ATT_QB, :] = km_ref[...]
    kx_ref[BLOCK + ATT_QB:, :] = kn_ref[...]
    vx_ref[0:BLOCK, :] = vp_ref[...]
    vx_ref[BLOCK:BLOCK + ATT_QB, :] = vm_ref[...]
    vx_ref[BLOCK + ATT_QB:, :] = vn_ref[...]

    qi = lax.broadcasted_iota(jnp.int32, (BLOCK, keys), 0)
    kj = lax.broadcasted_iota(jnp.int32, (BLOCK, keys), 1)
    wb_ref[...] = jnp.where(jnp.abs(kj - BLOCK - qi) <= WINDOW, 0.0, NEG_INF).astype(F32)

    logit_scale = (HEAD_DIM ** -0.5) * LOG2_E

    def sub_block(j, carry):
        r0 = pl.multiple_of(j * BLOCK, BLOCK)
        key_pos = (i * n_sub + j) * BLOCK - BLOCK + lax.broadcasted_iota(jnp.int32, (1, keys), 1)
        bias = wb_ref[...] + jnp.where((key_pos >= 0) & (key_pos < seq), 0.0, NEG_INF).astype(F32)
        for hh in range(N_KV_HEADS):
            kc = kx_ref[pl.ds(r0, keys), hh * HEAD_DIM:(hh + 1) * HEAD_DIM]
            vc = vx_ref[pl.ds(r0, keys), hh * HEAD_DIM:(hh + 1) * HEAD_DIM]
            qs = jnp.concatenate(
                [q_ref[pl.ds(r0, BLOCK), (hh * Q_PER_KV + g) * HEAD_DIM:(hh * Q_PER_KV + g + 1) * HEAD_DIM]
                 for g in range(Q_PER_KV)], axis=0)
            s = lax.dot_general(qs, kc, (((1,), (1,)), ((), ())), preferred_element_type=F32)
            probs, inv = [], []
            for g in range(Q_PER_KV):
                sk = sink_ref[hh * Q_PER_KV + g] * LOG2_E
                lg = s[g * BLOCK:(g + 1) * BLOCK, :] * logit_scale + bias
                m = jnp.maximum(jnp.max(lg, axis=-1, keepdims=True), sk)
                p = jnp.exp2(lg - m)
                inv.append(1.0 / (jnp.sum(p, axis=-1, keepdims=True) + jnp.exp2(sk - m)))
                probs.append(p.astype(BF16))
            pv = jnp.dot(jnp.concatenate(probs, axis=0), vc, preferred_element_type=F32)
            for g in range(Q_PER_KV):
                c0 = (hh * Q_PER_KV + g) * HEAD_DIM
                o_ref[pl.ds(r0, BLOCK), c0:c0 + HEAD_DIM] = (
                    pv[g * BLOCK:(g + 1) * BLOCK, :] * inv[g]).astype(o_ref.dtype)
        return carry

    lax.fori_loop(0, n_sub, sub_block, 0)


def _windowed_attention(qkv, sink_l, batch, seq):
    qkv3 = qkv.reshape(batch, seq, QKV_WIDTH)
    n_sub = ATT_QB // BLOCK
    n_blk = seq // BLOCK
    k_col = ATTN_WIDTH // KV_WIDTH
    v_col = k_col + 1

    def halo(col):
        prev = pl.BlockSpec((None, BLOCK, KV_WIDTH), lambda b, i: (b, jnp.maximum(i * n_sub - 1, 0), col))
        main = pl.BlockSpec((None, ATT_QB, KV_WIDTH), lambda b, i: (b, i, col))
        nxt = pl.BlockSpec((None, BLOCK, KV_WIDTH), lambda b, i: (b, jnp.minimum((i + 1) * n_sub, n_blk - 1), col))
        return [prev, main, nxt]

    out = pl.pallas_call(
        _attn_kernel,
        grid=(batch, seq // ATT_QB),
        in_specs=[pl.BlockSpec(memory_space=pltpu.SMEM),
                  pl.BlockSpec((None, ATT_QB, ATTN_WIDTH), lambda b, i: (b, i, 0))] + halo(k_col) + halo(v_col),
        out_specs=pl.BlockSpec((None, ATT_QB, ATTN_WIDTH), lambda b, i: (b, i, 0)),
        out_shape=jax.ShapeDtypeStruct((batch, seq, ATTN_WIDTH), BF16),
        scratch_shapes=[pltpu.VMEM((ATT_QB + 2 * BLOCK, KV_WIDTH), BF16),
                        pltpu.VMEM((ATT_QB + 2 * BLOCK, KV_WIDTH), BF16),
                        pltpu.VMEM((BLOCK, 3 * BLOCK), F32)],
        compiler_params=_params("parallel", "arbitrary"),
        name="window_attn",
    )(sink_l, qkv3, qkv3, qkv3, qkv3, qkv3, qkv3, qkv3)
    return out.reshape(batch * seq, ATTN_WIDTH)


def _conv_pool_kernel(yp_ref, ym_ref, yn_ref, pp_ref, pm_ref, pn_ref, wdw_ref, bdw_ref, lng_ref, lnb_ref,
                      band_ref, pw_ref, ps_ref, co_ref, po_ref, yx_ref, px_ref, sh_ref, cb_ref):
    i = pl.program_id(1)
    n_i = pl.num_programs(1)
    ts = ym_ref.shape[0]
    seq = n_i * ts

    has_prev = i > 0
    has_next = i < n_i - 1
    yx_ref[0:HALO, :] = jnp.where(has_prev, yp_ref[...], 0).astype(F32)
    yx_ref[HALO:HALO + ts, :] = ym_ref[...].astype(F32)
    yx_ref[HALO + ts:, :] = jnp.where(has_next, yn_ref[...], 0).astype(F32)
    px_ref[0:HALO, :] = jnp.where(has_prev, pp_ref[...], 0)
    px_ref[HALO:HALO + ts, :] = pm_ref[...]
    px_ref[HALO + ts:, :] = jnp.where(has_next, pn_ref[...], 0)

    n_sh = sh_ref.shape[1]
    for b in range(1, SUBLANES):
        sh_ref[b - 1] = yx_ref[b:b + n_sh, :]

    for r in range(0, ts, CP_ROWS):
        for c in range(0, CONV_WIDTH, LANES):
            acc = jnp.zeros((CP_ROWS, LANES), F32)
            for k in range(CONV_SIZE):
                a, b = divmod(k + HALO - CONV_PAD, SUBLANES)
                src = yx_ref if b == 0 else sh_ref.at[b - 1]
                r_src = r + a * SUBLANES
                acc = acc + src[r_src:r_src + CP_ROWS, c:c + LANES] * wdw_ref[k:k + 1, c:c + LANES]
            cb_ref[r:r + CP_ROWS, c:c + LANES] = acc + bdw_ref[:, c:c + LANES]

    x = cb_ref[...]
    mu = jnp.sum(x, axis=-1, keepdims=True) * (1.0 / CONV_WIDTH)
    xc = x - mu
    var = jnp.sum(xc * xc, axis=-1, keepdims=True) * (1.0 / CONV_WIDTH)
    yn = xc * lax.rsqrt(var + LN_EPS) * lng_ref[...] + lnb_ref[...]
    co_ref[...] = (yn * jax.nn.sigmoid(yn)).astype(co_ref.dtype)

    pos = i * ts + lax.broadcasted_iota(jnp.int32, (ts, 1), 0)
    for gi, size in enumerate(POOL_SIZES):
        half = size // 2
        c0 = gi * POOL_GROUP
        tot = jnp.dot(band_ref[gi], px_ref[:, c0:c0 + POOL_GROUP], preferred_element_type=F32)
        lo = jnp.clip(pos - half, 0, seq - 1)
        hi = jnp.clip(pos + half - 1, 0, seq - 1)
        cnt = (hi - lo + 1).astype(F32)
        mixed = (tot / cnt - pm_ref[:, c0:c0 + POOL_GROUP].astype(F32)).astype(BF16)
        yg = jnp.dot(mixed, pw_ref[gi], preferred_element_type=F32)
        po_ref[:, c0:c0 + POOL_GROUP] = (yg * ps_ref[:, c0:c0 + POOL_GROUP]).astype(po_ref.dtype)


def _pool_bands(ts):
    t = jnp.arange(ts)[:, None] + HALO
    r = jnp.arange(ts + 2 * HALO)[None, :]
    return jnp.stack([((r >= t - size // 2) & (r <= t + size // 2 - 1)).astype(BF16) for size in POOL_SIZES])


def _conv_pool(y, p, w_dw, b_dw, ln_g, ln_b, pool_w, pool_scale, l, batch, seq):
    y3 = y.reshape(batch, seq, CONV_WIDTH)
    p3 = p.reshape(batch, seq, POOL_WIDTH)
    per = CP_TS // HALO
    n_halo = seq // HALO
    ext = CP_TS + 2 * HALO

    def halo(width):
        prev = pl.BlockSpec((None, HALO, width), lambda b, i: (b, jnp.maximum(i * per - 1, 0), 0))
        main = pl.BlockSpec((None, CP_TS, width), lambda b, i: (b, i, 0))
        nxt = pl.BlockSpec((None, HALO, width), lambda b, i: (b, jnp.minimum((i + 1) * per, n_halo - 1), 0))
        return [prev, main, nxt]

    def layer(*shape):
        return pl.BlockSpec((None,) + shape, lambda b, i: (l,) + (0,) * len(shape))

    conv, pool = pl.pallas_call(
        _conv_pool_kernel,
        grid=(batch, seq // CP_TS),
        in_specs=halo(CONV_WIDTH) + halo(POOL_WIDTH) + [
            layer(CONV_SIZE, CONV_WIDTH), layer(1, CONV_WIDTH), layer(1, CONV_WIDTH), layer(1, CONV_WIDTH),
            pl.BlockSpec((N_POOL_GROUPS, CP_TS, ext), lambda b, i: (0, 0, 0)),
            layer(N_POOL_GROUPS, POOL_GROUP, POOL_GROUP), layer(1, POOL_WIDTH)],
        out_specs=[pl.BlockSpec((None, CP_TS, CONV_WIDTH), lambda b, i: (b, i, 0)),
                   pl.BlockSpec((None, CP_TS, POOL_WIDTH), lambda b, i: (b, i, 0))],
        out_shape=[jax.ShapeDtypeStruct((batch, seq, CONV_WIDTH), BF16),
                   jax.ShapeDtypeStruct((batch, seq, POOL_WIDTH), BF16)],
        scratch_shapes=[pltpu.VMEM((ext, CONV_WIDTH), F32),
                        pltpu.VMEM((ext, POOL_WIDTH), BF16),
                        pltpu.VMEM((SUBLANES - 1, ext - SUBLANES, CONV_WIDTH), F32),
                        pltpu.VMEM((CP_TS, CONV_WIDTH), F32)],
        compiler_params=_params("parallel", "arbitrary"),
        name="conv_pool",
    )(y3, y3, y3, p3, p3, p3, w_dw, b_dw[:, None, :], ln_g[:, None, :], ln_b[:, None, :],
      _pool_bands(CP_TS), pool_w, pool_scale[:, None, :])
    return conv.reshape(batch * seq, CONV_WIDTH), pool.reshape(batch * seq, POOL_WIDTH)


def _gated_merge_kernel(u_ref, at_ref, cv_ref, pl_ref, wg0_ref, wg1_ref, wg2_ref, wa_ref, wc_ref, wp_ref, o_ref):
    u = u_ref[...]

    def gate(w_ref):
        return jax.nn.sigmoid(jnp.dot(u, w_ref[...], preferred_element_type=F32))

    m = gate(wg0_ref) * jnp.dot(at_ref[...], wa_ref[...], preferred_element_type=F32)
    m = m + gate(wg1_ref) * jnp.dot(cv_ref[...], wc_ref[...], preferred_element_type=F32)
    m = m + gate(wg2_ref) * jnp.dot(pl_ref[...], wp_ref[...], preferred_element_type=F32)
    o_ref[...] = m.astype(o_ref.dtype)


def _gated_merge(u, attn, conv, pool, w_in, w_attn, w_conv, w_pool, l):
    t, d = u.shape
    row = lambda width: pl.BlockSpec((MERGE_TM, width), lambda i, n: (i, 0))
    return pl.pallas_call(
        _gated_merge_kernel,
        grid=(t // MERGE_TM, d // MERGE_TN),
        in_specs=[row(d), row(ATTN_WIDTH), row(CONV_WIDTH), row(POOL_WIDTH)]
        + [_layer_cols(l, d, MERGE_TN, COL_GATES + b * d) for b in range(N_BRANCHES)]
        + [_layer_cols(l, ATTN_WIDTH, MERGE_TN), _layer_cols(l, CONV_WIDTH, MERGE_TN),
           _layer_cols(l, POOL_WIDTH, MERGE_TN)],
        out_specs=pl.BlockSpec((MERGE_TM, MERGE_TN), lambda i, n: (i, n)),
        out_shape=jax.ShapeDtypeStruct((t, d), BF16),
        compiler_params=_params("parallel", "arbitrary"),
        name="gated_merge",
    )(u, attn, conv, pool, w_in, w_in, w_in, w_attn, w_conv, w_pool)


def _split_rotary_halves(a):
    shape = a.shape
    a = a.reshape(shape[:-1] + (shape[-1] // HEAD_DIM, HEAD_DIM))
    a = jnp.concatenate([a[..., :ROT_HALF], a[..., ROT_DIM:ROT_DIM + HEAD_DIM // 2 - ROT_HALF],
                         a[..., ROT_HALF:ROT_DIM], a[..., ROT_DIM + HEAD_DIM // 2 - ROT_HALF:]], axis=-1)
    return a.reshape(shape)


def _rope_lane_tables(seq):
    pos = jnp.arange(seq, dtype=F32)
    inv_freq = ROPE_THETA ** (-jnp.arange(0, ROT_DIM, 2, dtype=F32) / ROT_DIM)
    ang = pos[:, None] * inv_freq[None, :]
    cos, sin = jnp.cos(ang), jnp.sin(ang)
    ones = jnp.ones((seq, HEAD_DIM // 2 - ROT_HALF), F32)
    zeros = jnp.zeros((seq, HEAD_DIM // 2 - ROT_HALF), F32)
    cos_t = jnp.concatenate([cos, ones, cos, ones], axis=-1)
    sin_t = jnp.concatenate([-sin, zeros, sin, zeros], axis=-1)
    return cos_t, sin_t


def kernel(x, ffn1_pre_g, ffn1_post_g, ffn1_w_gate, ffn1_w_up, ffn1_w_down, mix_pre_g, mix_post_g, w_in, sink,
           w_attn_proj, conv_dw, conv_dw_b, conv_ln_g, conv_ln_b, w_conv_proj, pool_w, pool_scale, w_pool_proj,
           w_out, ffn2_pre_g, ffn2_post_g, ffn2_w_gate, ffn2_w_up, ffn2_w_down):
    batch, seq, d = x.shape
    cos_t, sin_t = _rope_lane_tables(seq)
    bf = lambda w: w.astype(BF16)
    qk_width = ATTN_WIDTH + KV_WIDTH
    w_qkv_b = bf(jnp.concatenate([_split_rotary_halves(w_in[:, :, :qk_width]), w_in[:, :, qk_width:QKV_WIDTH]], axis=-1))
    f1_gate, f1_up, f1_down = bf(ffn1_w_gate), bf(ffn1_w_up), bf(ffn1_w_down)
    f2_gate, f2_up, f2_down = bf(ffn2_w_gate), bf(ffn2_w_up), bf(ffn2_w_down)
    w_in_b, w_attn_b, w_conv_b, w_pool_b, w_out_b = bf(w_in), bf(w_attn_proj), bf(w_conv_proj), bf(w_pool_proj), bf(w_out)
    pool_w_b = bf(pool_w)

    h = x.reshape(batch * seq, d)
    xn = _rmsnorm_bf16(h, ffn1_pre_g[0])
    for l in range(DEPTH):
        act = _gate_up(xn, f1_gate, f1_up, l)
        h, u = _proj_residual(act, f1_down, l, h, ffn1_post_g[l], mix_pre_g[l], 0.5, DOWN_TM)
        qkv = _qkv_proj(u, w_qkv_b, l, cos_t, sin_t, seq)
        y, p = _glu_pool_proj(u, w_in_b, l)
        attn = _windowed_attention(qkv, sink[l], batch, seq)
        conv, pool = _conv_pool(y, p, conv_dw, conv_dw_b, conv_ln_g, conv_ln_b, pool_w_b, pool_scale, l, batch, seq)
        merged = _gated_merge(u, attn, conv, pool, w_in_b, w_attn_b, w_conv_b, w_pool_b, l)
        h, xn = _proj_residual(merged, w_out_b, l, h, mix_post_g[l], ffn2_pre_g[l], 1.0, OUT_TM)
        act = _gate_up(xn, f2_gate, f2_up, l)
        next_g = ffn1_pre_g[l + 1] if l + 1 < DEPTH else None
        h, xn = _proj_residual(act, f2_down, l, h, ffn2_post_g[l], next_g, 0.5, DOWN_TM)
    return h.reshape(batch, seq, d)
```

```python
import functools
import math

import jax
import jax.numpy as jnp
from jax import lax
from jax.experimental import pallas as pl
from jax.experimental.pallas import tpu as pltpu

D_MODEL = 2048
DEPTH = 4
HEAD_DIM = 128
N_Q_HEADS = D_MODEL // HEAD_DIM
N_KV_HEADS = N_Q_HEADS // 4
Q_PER_KV = N_Q_HEADS // N_KV_HEADS
ATTN_WIDTH = N_Q_HEADS * HEAD_DIM
KV_WIDTH = N_KV_HEADS * HEAD_DIM
WINDOW = 128
BLOCK = 128
ROPE_THETA = 500000.0
ROT_DIM = HEAD_DIM // 4
ROT_HALF = ROT_DIM // 2
CONV_WIDTH = D_MODEL // 2
CONV_SIZE = 31
CONV_PAD = CONV_SIZE // 2
POOL_WIDTH = D_MODEL // 2
POOL_SIZES = (2, 4, 8, 16)
N_POOL_GROUPS = len(POOL_SIZES)
POOL_GROUP = POOL_WIDTH // N_POOL_GROUPS
N_BRANCHES = 3
D_FF = ((8 * D_MODEL // 3 + 255) // 256) * 256
RMS_EPS = 1e-6
LN_EPS = 1e-5
NEG_INF = -1e30
LOG2_E = math.log2(math.e)

QKV_WIDTH = ATTN_WIDTH + 2 * KV_WIDTH
COL_CONV_A = QKV_WIDTH
COL_CONV_G = COL_CONV_A + CONV_WIDTH
COL_POOL = COL_CONV_G + CONV_WIDTH
COL_GATES = COL_POOL + POOL_WIDTH

LANES = 128
SUBLANES = 8
BF16_SUBLANES = 16
MXU_COLS = 256
VMEM_LIMIT_BYTES = 56 * 1024 * 1024

NORM_ROWS = 512
UP_TM = 1024
UP_TF = 512
DOWN_TM = 256
OUT_TM = 512
RES_ROWS = 128
PROJ_TM = 1024
PROJ_TN = 512
ATT_QB = 1024
HALO = 16
CP_TS = 256
CP_ROWS = 64
MERGE_TM = 1024
MERGE_TN = 256

F32 = jnp.float32
BF16 = jnp.bfloat16


def _params(*semantics):
    return pltpu.CompilerParams(dimension_semantics=semantics, vmem_limit_bytes=VMEM_LIMIT_BYTES)


def _rms_scale(x):
    return lax.rsqrt(jnp.sum(x * x, axis=-1, keepdims=True) * (1.0 / x.shape[-1]) + RMS_EPS)


def _layer_cols(l, k, tn, col0=0):
    return pl.BlockSpec((None, k, tn), lambda i, j: (l, 0, col0 // tn + j))


def _rmsnorm_kernel(h_ref, g_ref, o_ref):
    x = h_ref[...]
    o_ref[...] = (x * _rms_scale(x) * g_ref[...]).astype(o_ref.dtype)


def _rmsnorm_bf16(h, g):
    t, d = h.shape
    return pl.pallas_call(
        _rmsnorm_kernel,
        grid=(t // NORM_ROWS,),
        in_specs=[pl.BlockSpec((NORM_ROWS, d), lambda i: (i, 0)),
                  pl.BlockSpec((1, d), lambda i: (0, 0))],
        out_specs=pl.BlockSpec((NORM_ROWS, d), lambda i: (i, 0)),
        out_shape=jax.ShapeDtypeStruct((t, d), BF16),
        compiler_params=_params("parallel"),
        name="rmsnorm",
    )(h, g.reshape(1, d))


def _gate_up_kernel(x_ref, wg_ref, wu_ref, o_ref, wg_bf_ref, wu_bf_ref):
    @pl.when(pl.program_id(1) == 0)
    def _():
        wg_bf_ref[...] = wg_ref[...].astype(BF16)
        wu_bf_ref[...] = wu_ref[...].astype(BF16)

    x = x_ref[...]
    g = jnp.dot(x, wg_bf_ref[...], preferred_element_type=F32)
    u = jnp.dot(x, wu_bf_ref[...], preferred_element_type=F32)
    o_ref[...] = (g * jax.nn.sigmoid(g) * u).astype(o_ref.dtype)


def _gate_up(xn, w_gate, w_up, l):
    t, d = xn.shape
    ff = w_gate.shape[2]
    w_spec = pl.BlockSpec((None, d, UP_TF), lambda j, i: (l, 0, j))
    return pl.pallas_call(
        _gate_up_kernel,
        grid=(ff // UP_TF, t // UP_TM),
        in_specs=[pl.BlockSpec((UP_TM, d), lambda j, i: (i, 0)), w_spec, w_spec],
        out_specs=pl.BlockSpec((UP_TM, UP_TF), lambda j, i: (i, j)),
        out_shape=jax.ShapeDtypeStruct((t, ff), BF16),
        scratch_shapes=[pltpu.VMEM((d, UP_TF), BF16), pltpu.VMEM((d, UP_TF), BF16)],
        compiler_params=_params("arbitrary", "arbitrary"),
        name="gate_up",
    )(xn, w_gate, w_up)


def _proj_residual_kernel(a_ref, w_ref, h_ref, post_g_ref, *rest, res_scale, emit_next):
    if emit_next:
        next_g_ref, o_ref, nx_ref, z_even_ref, z_odd_ref = rest
    else:
        o_ref, z_even_ref, z_odd_ref = rest
    i = pl.program_id(0)
    tm = o_ref.shape[0]

    @pl.when(i == 0)
    def _():
        z_odd_ref[...] = jnp.zeros(z_odd_ref.shape, F32)

    def step(z_cur_ref, z_prev_ref):
        z_cur_ref[...] = jnp.dot(a_ref[...], w_ref[...], preferred_element_type=F32)
        for r in range(0, tm, RES_ROWS):
            y = z_prev_ref[r:r + RES_ROWS, :]
            hn = h_ref[r:r + RES_ROWS, :] + res_scale * (y * _rms_scale(y) * post_g_ref[...])
            o_ref[r:r + RES_ROWS, :] = hn
            if emit_next:
                nx_ref[r:r + RES_ROWS, :] = (hn * _rms_scale(hn) * next_g_ref[...]).astype(nx_ref.dtype)

    @pl.when(i % 2 == 0)
    def _():
        step(z_even_ref, z_odd_ref)

    @pl.when(i % 2 == 1)
    def _():
        step(z_odd_ref, z_even_ref)


def _proj_residual(a, w, l, h, post_g, next_g, res_scale, tm):
    t, d = h.shape
    k = a.shape[1]
    n_tiles = t // tm
    emit_next = next_g is not None
    vec = pl.BlockSpec((1, d), lambda i: (0, 0))
    lagged = pl.BlockSpec((tm, d), lambda i: (jnp.maximum(i - 1, 0), 0))
    in_specs = [pl.BlockSpec((tm, k), lambda i: (jnp.minimum(i, n_tiles - 1), 0)),
                pl.BlockSpec((None, k, d), lambda i: (l, 0, 0), pipeline_mode=pl.Buffered(1)),
                lagged, vec]
    args = [a, w, h, post_g.reshape(1, d)]
    out_specs, out_shape = [lagged], [jax.ShapeDtypeStruct((t, d), F32)]
    if emit_next:
        in_specs.append(vec)
        args.append(next_g.reshape(1, d))
        out_specs.append(lagged)
        out_shape.append(jax.ShapeDtypeStruct((t, d), BF16))
    outs = pl.pallas_call(
        functools.partial(_proj_residual_kernel, res_scale=res_scale, emit_next=emit_next),
        grid=(n_tiles + 1,),
        in_specs=in_specs, out_specs=out_specs, out_shape=out_shape,
        scratch_shapes=[pltpu.VMEM((tm, d), F32), pltpu.VMEM((tm, d), F32)],
        compiler_params=_params("arbitrary"),
        name="proj_residual",
    )(*args)
    return (outs[0], outs[1]) if emit_next else (outs[0], None)


def _qkv_kernel(u_ref, w_ref, cos_ref, sin_ref, o_ref):
    j = pl.program_id(1)
    is_rot = j < (ATTN_WIDTH + KV_WIDTH) // PROJ_TN
    c = jnp.where(is_rot, cos_ref[...], 1.0)
    s = jnp.where(is_rot, sin_ref[...], 0.0)
    u = u_ref[...]
    for n in range(0, PROJ_TN, MXU_COLS):
        z = jnp.dot(u, w_ref[:, n:n + MXU_COLS], preferred_element_type=F32)
        for a in range(0, MXU_COLS, HEAD_DIM):
            x = z[:, a:a + HEAD_DIM]
            partner = pltpu.roll(x, HEAD_DIM // 2, axis=1)
            o_ref[:, n + a:n + a + HEAD_DIM] = (x * c + partner * s).astype(o_ref.dtype)


def _qkv_proj(u, w_qkv, l, cos_t, sin_t, seq):
    t, d = u.shape
    s_tiles = seq // PROJ_TM
    tab = pl.BlockSpec((PROJ_TM, HEAD_DIM), lambda i, j: (i % s_tiles, 0))
    return pl.pallas_call(
        _qkv_kernel,
        grid=(t // PROJ_TM, QKV_WIDTH // PROJ_TN),
        in_specs=[pl.BlockSpec((PROJ_TM, d), lambda i, j: (i, 0)),
                  _layer_cols(l, d, PROJ_TN), tab, tab],
        out_specs=pl.BlockSpec((PROJ_TM, PROJ_TN), lambda i, j: (i, j)),
        out_shape=jax.ShapeDtypeStruct((t, QKV_WIDTH), BF16),
        compiler_params=_params("parallel", "arbitrary"),
        name="qkv_proj",
    )(u, w_qkv, cos_t, sin_t)


def _glu_pool_kernel(u_ref, wa_ref, wg_ref, wp_ref, y_ref, p_ref):
    u = u_ref[...]
    a = jnp.dot(u, wa_ref[...], preferred_element_type=F32)
    g = jnp.dot(u, wg_ref[...], preferred_element_type=F32)
    y_ref[...] = (a * jax.nn.sigmoid(g)).astype(y_ref.dtype)
    p_ref[...] = jnp.dot(u, wp_ref[...], preferred_element_type=F32).astype(p_ref.dtype)


def _glu_pool_proj(u, w_in, l):
    t, d = u.shape
    out = pl.BlockSpec((PROJ_TM, PROJ_TN), lambda i, j: (i, j))
    return pl.pallas_call(
        _glu_pool_kernel,
        grid=(t // PROJ_TM, CONV_WIDTH // PROJ_TN),
        in_specs=[pl.BlockSpec((PROJ_TM, d), lambda i, j: (i, 0)),
                  _layer_cols(l, d, PROJ_TN, COL_CONV_A),
                  _layer_cols(l, d, PROJ_TN, COL_CONV_G),
                  _layer_cols(l, d, PROJ_TN, COL_POOL)],
        out_specs=[out, out],
        out_shape=[jax.ShapeDtypeStruct((t, CONV_WIDTH), BF16),
                   jax.ShapeDtypeStruct((t, POOL_WIDTH), BF16)],
        compiler_params=_params("parallel", "arbitrary"),
        name="glu_pool_proj",
    )(u, w_in, w_in, w_in)


def _attn_kernel(sink_ref, q_ref, kp_ref, km_ref, kn_ref, vp_ref, vm_ref, vn_ref, o_ref,
                 kx_ref, vx_ref, wb_ref):
    i = pl.program_id(1)
    n_sub = ATT_QB // BLOCK
    seq = pl.num_programs(1) * ATT_QB
    keys = 3 * BLOCK

    kx_ref[0:BLOCK, :] = kp_ref[...]
    kx_ref[BLOCK:BLOCK + ATT_QB, :] = km_ref[...]
    kx_ref[BLOCK + ATT_QB:, :] = kn_ref[...]
    vx_ref[0:BLOCK, :] = vp_ref[...]
    vx_ref[BLOCK:BLOCK + ATT_QB, :] = vm_ref[...]
    vx_ref[BLOCK + ATT_QB:, :] = vn_ref[...]

    qi = lax.broadcasted_iota(jnp.int32, (BLOCK, keys), 0)
    kj = lax.broadcasted_iota(jnp.int32, (BLOCK, keys), 1)
    wb_ref[...] = jnp.where(jnp.abs(kj - BLOCK - qi) <= WINDOW, 0.0, NEG_INF).astype(F32)

    logit_scale = (HEAD_DIM ** -0.5) * LOG2_E

    def sub_block(j, carry):
        r0 = pl.multiple_of(j * BLOCK, BLOCK)
        key_pos = (i * n_sub + j) * BLOCK - BLOCK + lax.broadcasted_iota(jnp.int32, (1, keys), 1)
        bias = wb_ref[...] + jnp.where((key_pos >= 0) & (key_pos < seq), 0.0, NEG_INF).astype(F32)
        for hh in range(N_KV_HEADS):
            kc = kx_ref[pl.ds(r0, keys), hh * HEAD_DIM:(hh + 1) * HEAD_DIM]
            vc = vx_ref[pl.ds(r0, keys), hh * HEAD_DIM:(hh + 1) * HEAD_DIM]
            qs = jnp.concatenate(
                [q_ref[pl.ds(r0, BLOCK), (hh * Q_PER_KV + g) * HEAD_DIM:(hh * Q_PER_KV + g + 1) * HEAD_DIM]
                 for g in range(Q_PER_KV)], axis=0)
            s = lax.dot_general(qs, kc, (((1,), (1,)), ((), ())), preferred_element_type=F32)
            probs, inv = [], []
            for g in range(Q_PER_KV):
                sk = sink_ref[hh * Q_PER_KV + g] * LOG2_E
                lg = s[g * BLOCK:(g + 1) * BLOCK, :] * logit_scale + bias
                m = jnp.maximum(jnp.max(lg, axis=-1, keepdims=True), sk)
                p = jnp.exp2(lg - m)
                inv.append(1.0 / (jnp.sum(p, axis=-1, keepdims=True) + jnp.exp2(sk - m)))
                probs.append(p.astype(BF16))
            pv = jnp.dot(jnp.concatenate(probs, axis=0), vc, preferred_element_type=F32)
            for g in range(Q_PER_KV):
                c0 = (hh * Q_PER_KV + g) * HEAD_DIM
                o_ref[pl.ds(r0, BLOCK), c0:c0 + HEAD_DIM] = (
                    pv[g * BLOCK:(g + 1) * BLOCK, :] * inv[g]).astype(o_ref.dtype)
        return carry

    lax.fori_loop(0, n_sub, sub_block, 0)


def _windowed_attention(qkv, sink_l, batch, seq):
    qkv3 = qkv.reshape(batch, seq, QKV_WIDTH)
    n_sub = ATT_QB // BLOCK
    n_blk = seq // BLOCK
    k_col = ATTN_WIDTH // KV_WIDTH
    v_col = k_col + 1

    def halo(col):
        prev = pl.BlockSpec((None, BLOCK, KV_WIDTH), lambda b, i: (b, jnp.maximum(i * n_sub - 1, 0), col))
        main = pl.BlockSpec((None, ATT_QB, KV_WIDTH), lambda b, i: (b, i, col))
        nxt = pl.BlockSpec((None, BLOCK, KV_WIDTH), lambda b, i: (b, jnp.minimum((i + 1) * n_sub, n_blk - 1), col))
        return [prev, main, nxt]

    out = pl.pallas_call(
        _attn_kernel,
        grid=(batch, seq // ATT_QB),
        in_specs=[pl.BlockSpec(memory_space=pltpu.SMEM),
                  pl.BlockSpec((None, ATT_QB, ATTN_WIDTH), lambda b, i: (b, i, 0))] + halo(k_col) + halo(v_col),
        out_specs=pl.BlockSpec((None, ATT_QB, ATTN_WIDTH), lambda b, i: (b, i, 0)),
        out_shape=jax.ShapeDtypeStruct((batch, seq, ATTN_WIDTH), BF16),
        scratch_shapes=[pltpu.VMEM((ATT_QB + 2 * BLOCK, KV_WIDTH), BF16),
                        pltpu.VMEM((ATT_QB + 2 * BLOCK, KV_WIDTH), BF16),
                        pltpu.VMEM((BLOCK, 3 * BLOCK), F32)],
        compiler_params=_params("parallel", "arbitrary"),
        name="window_attn",
    )(sink_l, qkv3, qkv3, qkv3, qkv3, qkv3, qkv3, qkv3)
    return out.reshape(batch * seq, ATTN_WIDTH)


def _conv_pool_kernel(yp_ref, ym_ref, yn_ref, pp_ref, pm_ref, pn_ref, wdw_ref, bdw_ref, lng_ref, lnb_ref,
                      band_ref, pw_ref, ps_ref, co_ref, po_ref, yx_ref, px_ref, sh_ref, cb_ref):
    i = pl.program_id(1)
    n_i = pl.num_programs(1)
    ts = ym_ref.shape[0]
    seq = n_i * ts

    has_prev = i > 0
    has_next = i < n_i - 1
    yx_ref[0:HALO, :] = jnp.where(has_prev, yp_ref[...], 0).astype(F32)
    yx_ref[HALO:HALO + ts, :] = ym_ref[...].astype(F32)
    yx_ref[HALO + ts:, :] = jnp.where(has_next, yn_ref[...], 0).astype(F32)
    px_ref[0:HALO, :] = jnp.where(has_prev, pp_ref[...], 0)
    px_ref[HALO:HALO + ts, :] = pm_ref[...]
    px_ref[HALO + ts:, :] = jnp.where(has_next, pn_ref[...], 0)

    n_sh = sh_ref.shape[1]
    for b in range(1, SUBLANES):
        sh_ref[b - 1] = yx_ref[b:b + n_sh, :]

    for r in range(0, ts, CP_ROWS):
        for c in range(0, CONV_WIDTH, LANES):
            acc = jnp.zeros((CP_ROWS, LANES), F32)
            for k in range(CONV_SIZE):
                a, b = divmod(k + HALO - CONV_PAD, SUBLANES)
                src = yx_ref if b == 0 else sh_ref.at[b - 1]
                r_src = r + a * SUBLANES
                acc = acc + src[r_src:r_src + CP_ROWS, c:c + LANES] * wdw_ref[k:k + 1, c:c + LANES]
            cb_ref[r:r + CP_ROWS, c:c + LANES] = acc + bdw_ref[:, c:c + LANES]

    x = cb_ref[...]
    mu = jnp.sum(x, axis=-1, keepdims=True) * (1.0 / CONV_WIDTH)
    xc = x - mu
    var = jnp.sum(xc * xc, axis=-1, keepdims=True) * (1.0 / CONV_WIDTH)
    yn = xc * lax.rsqrt(var + LN_EPS) * lng_ref[...] + lnb_ref[...]
    co_ref[...] = (yn * jax.nn.sigmoid(yn)).astype(co_ref.dtype)

    pos = i * ts + lax.broadcasted_iota(jnp.int32, (ts, 1), 0)
    for gi, size in enumerate(POOL_SIZES):
        half = size // 2
        c0 = gi * POOL_GROUP
        tot = jnp.dot(band_ref[gi], px_ref[:, c0:c0 + POOL_GROUP], preferred_element_type=F32)
        lo = jnp.clip(pos - half, 0, seq - 1)
        hi = jnp.clip(pos + half - 1, 0, seq - 1)
        cnt = (hi - lo + 1).astype(F32)
        mixed = (tot / cnt - pm_ref[:, c0:c0 + POOL_GROUP].astype(F32)).astype(BF16)
        yg = jnp.dot(mixed, pw_ref[gi], preferred_element_type=F32)
        po_ref[:, c0:c0 + POOL_GROUP] = (yg * ps_ref[:, c0:c0 + POOL_GROUP]).astype(po_ref.dtype)


def _pool_bands(ts):
    t = jnp.arange(ts)[:, None] + HALO
    r = jnp.arange(ts + 2 * HALO)[None, :]
    return jnp.stack([((r >= t - size // 2) & (r <= t + size // 2 - 1)).astype(BF16) for size in POOL_SIZES])


def _conv_pool(y, p, w_dw, b_dw, ln_g, ln_b, pool_w, pool_scale, l, batch, seq):
    y3 = y.reshape(batch, seq, CONV_WIDTH)
    p3 = p.reshape(batch, seq, POOL_WIDTH)
    per = CP_TS // HALO
    n_halo = seq // HALO
    ext = CP_TS + 2 * HALO

    def halo(width):
        prev = pl.BlockSpec((None, HALO, width), lambda b, i: (b, jnp.maximum(i * per - 1, 0), 0))
        main = pl.BlockSpec((None, CP_TS, width), lambda b, i: (b, i, 0))
        nxt = pl.BlockSpec((None, HALO, width), lambda b, i: (b, jnp.minimum((i + 1) * per, n_halo - 1), 0))
        return [prev, main, nxt]

    def layer(*shape):
        return pl.BlockSpec((None,) + shape, lambda b, i: (l,) + (0,) * len(shape))

    conv, pool = pl.pallas_call(
        _conv_pool_kernel,
        grid=(batch, seq // CP_TS),
        in_specs=halo(CONV_WIDTH) + halo(POOL_WIDTH) + [
            layer(CONV_SIZE, CONV_WIDTH), layer(1, CONV_WIDTH), layer(1, CONV_WIDTH), layer(1, CONV_WIDTH),
            pl.BlockSpec((N_POOL_GROUPS, CP_TS, ext), lambda b, i: (0, 0, 0)),
            layer(N_POOL_GROUPS, POOL_GROUP, POOL_GROUP), layer(1, POOL_WIDTH)],
        out_specs=[pl.BlockSpec((None, CP_TS, CONV_WIDTH), lambda b, i: (b, i, 0)),
                   pl.BlockSpec((None, CP_TS, POOL_WIDTH), lambda b, i: (b, i, 0))],
        out_shape=[jax.ShapeDtypeStruct((batch, seq, CONV_WIDTH), BF16),
                   jax.ShapeDtypeStruct((batch, seq, POOL_WIDTH), BF16)],
        scratch_shapes=[pltpu.VMEM((ext, CONV_WIDTH), F32),
                        pltpu.VMEM((ext, POOL_WIDTH), BF16),
                        pltpu.VMEM((SUBLANES - 1, ext - SUBLANES, CONV_WIDTH), F32),
                        pltpu.VMEM((CP_TS, CONV_WIDTH), F32)],
        compiler_params=_params("parallel", "arbitrary"),
        name="conv_pool",
    )(y3, y3, y3, p3, p3, p3, w_dw, b_dw[:, None, :], ln_g[:, None, :], ln_b[:, None, :],
      _pool_bands(CP_TS), pool_w, pool_scale[:, None, :])
    return conv.reshape(batch * seq, CONV_WIDTH), pool.reshape(batch * seq, POOL_WIDTH)


def _gated_merge_kernel(u_ref, at_ref, cv_ref, pl_ref, wg0_ref, wg1_ref, wg2_ref, wa_ref, wc_ref, wp_ref, o_ref):
    u = u_ref[...]

    def gate(w_ref):
        return jax.nn.sigmoid(jnp.dot(u, w_ref[...], preferred_element_type=F32))

    m = gate(wg0_ref) * jnp.dot(at_ref[...], wa_ref[...], preferred_element_type=F32)
    m = m + gate(wg1_ref) * jnp.dot(cv_ref[...], wc_ref[...], preferred_element_type=F32)
    m = m + gate(wg2_ref) * jnp.dot(pl_ref[...], wp_ref[...], preferred_element_type=F32)
    o_ref[...] = m.astype(o_ref.dtype)


def _gated_merge(u, attn, conv, pool, w_in, w_attn, w_conv, w_pool, l):
    t, d = u.shape
    row = lambda width: pl.BlockSpec((MERGE_TM, width), lambda i, n: (i, 0))
    return pl.pallas_call(
        _gated_merge_kernel,
        grid=(t // MERGE_TM, d // MERGE_TN),
        in_specs=[row(d), row(ATTN_WIDTH), row(CONV_WIDTH), row(POOL_WIDTH)]
        + [_layer_cols(l, d, MERGE_TN, COL_GATES + b * d) for b in range(N_BRANCHES)]
        + [_layer_cols(l, ATTN_WIDTH, MERGE_TN), _layer_cols(l, CONV_WIDTH, MERGE_TN),
           _layer_cols(l, POOL_WIDTH, MERGE_TN)],
        out_specs=pl.BlockSpec((MERGE_TM, MERGE_TN), lambda i, n: (i, n)),
        out_shape=jax.ShapeDtypeStruct((t, d), BF16),
        compiler_params=_params("parallel", "arbitrary"),
        name="gated_merge",
    )(u, attn, conv, pool, w_in, w_in, w_in, w_attn, w_conv, w_pool)


def _split_rotary_halves(a):
    shape = a.shape
    a = a.reshape(shape[:-1] + (shape[-1] // HEAD_DIM, HEAD_DIM))
    a = jnp.concatenate([a[..., :ROT_HALF], a[..., ROT_DIM:ROT_DIM + HEAD_DIM // 2 - ROT_HALF],
                         a[..., ROT_HALF:ROT_DIM], a[..., ROT_DIM + HEAD_DIM // 2 - ROT_HALF:]], axis=-1)
    return a.reshape(shape)


def _rope_lane_tables(seq):
    pos = jnp.arange(seq, dtype=F32)
    inv_freq = ROPE_THETA ** (-jnp.arange(0, ROT_DIM, 2, dtype=F32) / ROT_DIM)
    ang = pos[:, None] * inv_freq[None, :]
    cos, sin = jnp.cos(ang), jnp.sin(ang)
    ones = jnp.ones((seq, HEAD_DIM // 2 - ROT_HALF), F32)
    zeros = jnp.zeros((seq, HEAD_DIM // 2 - ROT_HALF), F32)
    cos_t = jnp.concatenate([cos, ones, cos, ones], axis=-1)
    sin_t = jnp.concatenate([-sin, zeros, sin, zeros], axis=-1)
    return cos_t, sin_t


def kernel(x, ffn1_pre_g, ffn1_post_g, ffn1_w_gate, ffn1_w_up, ffn1_w_down, mix_pre_g, mix_post_g, w_in, sink,
           w_attn_proj, conv_dw, conv_dw_b, conv_ln_g, conv_ln_b, w_conv_proj, pool_w, pool_scale, w_pool_proj,
           w_out, ffn2_pre_g, ffn2_post_g, ffn2_w_gate, ffn2_w_up, ffn2_w_down):
    batch, seq, d = x.shape
    cos_t, sin_t = _rope_lane_tables(seq)
    bf = lambda w: w.astype(BF16)
    qk_width = ATTN_WIDTH + KV_WIDTH
    w_qkv_b = bf(jnp.concatenate([_split_rotary_halves(w_in[:, :, :qk_width]), w_in[:, :, qk_width:QKV_WIDTH]], axis=-1))
    f1_down, f2_down = bf(ffn1_w_down), bf(ffn2_w_down)
    w_in_b, w_attn_b, w_conv_b, w_pool_b, w_out_b = bf(w_in), bf(w_attn_proj), bf(w_conv_proj), bf(w_pool_proj), bf(w_out)
    pool_w_b = bf(pool_w)

    h = x.reshape(batch * seq, d)
    xn = _rmsnorm_bf16(h, ffn1_pre_g[0])
    for l in range(DEPTH):
        act = _gate_up(xn, ffn1_w_gate, ffn1_w_up, l)
        h, u = _proj_residual(act, f1_down, l, h, ffn1_post_g[l], mix_pre_g[l], 0.5, DOWN_TM)
        qkv = _qkv_proj(u, w_qkv_b, l, cos_t, sin_t, seq)
        y, p = _glu_pool_proj(u, w_in_b, l)
        attn = _windowed_attention(qkv, sink[l], batch, seq)
        conv, pool = _conv_pool(y, p, conv_dw, conv_dw_b, conv_ln_g, conv_ln_b, pool_w_b, pool_scale, l, batch, seq)
        merged = _gated_merge(u, attn, conv, pool, w_in_b, w_attn_b, w_conv_b, w_pool_b, l)
        h, xn = _proj_residual(merged, w_out_b, l, h, mix_post_g[l], ffn2_pre_g[l], 1.0, OUT_TM)
        act = _gate_up(xn, ffn2_w_gate, ffn2_w_up, l)
        next_g = ffn1_pre_g[l + 1] if l + 1 < DEPTH else None
        h, xn = _proj_residual(act, f2_down, l, h, ffn2_post_g[l], next_g, 0.5, DOWN_TM)
    return h.reshape(batch, seq, d)
```

```python
import functools
import math

import jax
import jax.numpy as jnp
from jax import lax
from jax.experimental import pallas as pl
from jax.experimental.pallas import tpu as pltpu

D_MODEL = 2048
DEPTH = 4
HEAD_DIM = 128
N_Q_HEADS = D_MODEL // HEAD_DIM
N_KV_HEADS = N_Q_HEADS // 4
Q_PER_KV = N_Q_HEADS // N_KV_HEADS
ATTN_WIDTH = N_Q_HEADS * HEAD_DIM
KV_WIDTH = N_KV_HEADS * HEAD_DIM
WINDOW = 128
BLOCK = 128
ROPE_THETA = 500000.0
ROT_DIM = HEAD_DIM // 4
ROT_HALF = ROT_DIM // 2
CONV_WIDTH = D_MODEL // 2
CONV_SIZE = 31
CONV_PAD = CONV_SIZE // 2
POOL_WIDTH = D_MODEL // 2
POOL_SIZES = (2, 4, 8, 16)
N_POOL_GROUPS = len(POOL_SIZES)
POOL_GROUP = POOL_WIDTH // N_POOL_GROUPS
N_BRANCHES = 3
RMS_EPS = 1e-6
LN_EPS = 1e-5
NEG_INF = -1e30
LOG2_E = math.log2(math.e)

QKV_WIDTH = ATTN_WIDTH + 2 * KV_WIDTH
COL_CONV_A = QKV_WIDTH
COL_CONV_G = COL_CONV_A + CONV_WIDTH
COL_POOL = COL_CONV_G + CONV_WIDTH
COL_GATES = COL_POOL + POOL_WIDTH

LANES = 128
SUBLANES = 8
MXU_COLS = 256
VMEM_LIMIT_BYTES = 56 * 1024 * 1024

NORM_ROWS = 512
UP_TM = 1024
UP_TF = 512
DOWN_TM = 256
OUT_TM = 512
RES_ROWS = 128
PROJ_TM = 1024
PROJ_TN = 512
ATT_QB = 1024
HALO = 16
CP_TS = 256
CP_ROWS = 64
MERGE_TM = 1024
MERGE_TN = 256

F32 = jnp.float32
BF16 = jnp.bfloat16


def _params(*semantics):
    return pltpu.CompilerParams(dimension_semantics=semantics, vmem_limit_bytes=VMEM_LIMIT_BYTES)


def _rms_scale(x):
    return lax.rsqrt(jnp.sum(x * x, axis=-1, keepdims=True) * (1.0 / x.shape[-1]) + RMS_EPS)


def _cols(k, tn, col0=0):
    return pl.BlockSpec((k, tn), lambda i, j: (0, col0 // tn + j))


def _rmsnorm_kernel(h_ref, g_ref, o_ref):
    x = h_ref[...]
    o_ref[...] = (x * _rms_scale(x) * g_ref[...]).astype(o_ref.dtype)


def _rmsnorm_bf16(h, g):
    t, d = h.shape
    return pl.pallas_call(
        _rmsnorm_kernel,
        grid=(t // NORM_ROWS,),
        in_specs=[pl.BlockSpec((NORM_ROWS, d), lambda i: (i, 0)),
                  pl.BlockSpec((1, d), lambda i: (0, 0))],
        out_specs=pl.BlockSpec((NORM_ROWS, d), lambda i: (i, 0)),
        out_shape=jax.ShapeDtypeStruct((t, d), BF16),
        compiler_params=_params("parallel"),
        name="rmsnorm",
    )(h, g.reshape(1, d))


def _gate_up_kernel(x_ref, wg_ref, wu_ref, wd_ref, o_ref, wd_bf_ref, wg_bf_ref, wu_bf_ref):
    @pl.when(pl.program_id(1) == 0)
    def _():
        wg_bf_ref[...] = wg_ref[...].astype(BF16)
        wu_bf_ref[...] = wu_ref[...].astype(BF16)
        wd_bf_ref[...] = wd_ref[...].astype(BF16)

    x = x_ref[...]
    g = jnp.dot(x, wg_bf_ref[...], preferred_element_type=F32)
    u = jnp.dot(x, wu_bf_ref[...], preferred_element_type=F32)
    o_ref[...] = (g * jax.nn.sigmoid(g) * u).astype(o_ref.dtype)


def _gate_up(xn, w_gate, w_up, w_down, l):
    t, d = xn.shape
    ff = w_gate.shape[2]
    w_spec = pl.BlockSpec((None, d, UP_TF), lambda j, i: (l, 0, j))
    return pl.pallas_call(
        _gate_up_kernel,
        grid=(ff // UP_TF, t // UP_TM),
        in_specs=[pl.BlockSpec((UP_TM, d), lambda j, i: (i, 0)), w_spec, w_spec,
                  pl.BlockSpec((None, UP_TF, d), lambda j, i: (l, j, 0))],
        out_specs=[pl.BlockSpec((UP_TM, UP_TF), lambda j, i: (i, j)),
                   pl.BlockSpec((UP_TF, d), lambda j, i: (j, 0))],
        out_shape=[jax.ShapeDtypeStruct((t, ff), BF16), jax.ShapeDtypeStruct((ff, d), BF16)],
        scratch_shapes=[pltpu.VMEM((d, UP_TF), BF16), pltpu.VMEM((d, UP_TF), BF16)],
        compiler_params=_params("arbitrary", "arbitrary"),
        name="gate_up",
    )(xn, w_gate, w_up, w_down)


def _rotary_layout_block(x):
    lane = lax.broadcasted_iota(jnp.int32, (x.shape[0], HEAD_DIM), 1)
    from_above = (lane >= ROT_HALF) & (lane < HEAD_DIM // 2)
    from_below = (lane >= HEAD_DIM // 2) & (lane < HEAD_DIM // 2 + ROT_HALF)
    heads = []
    for c in range(0, x.shape[1], HEAD_DIM):
        xh = x[:, c:c + HEAD_DIM]
        up = pltpu.roll(xh, HEAD_DIM - ROT_HALF, axis=1)
        down = pltpu.roll(xh, HEAD_DIM // 2 - ROT_HALF, axis=1)
        heads.append(jnp.where(from_above, up, jnp.where(from_below, down, xh)))
    return jnp.concatenate(heads, axis=1)


def _proj_residual_kernel(a_ref, w_ref, h_ref, post_g_ref, *rest, res_scale, emit_next, n_cast, emit_qkv):
    rest = list(rest)
    next_g_ref = rest.pop(0) if emit_next else None
    cast_src_refs = [rest.pop(0) for _ in range(n_cast)]
    o_ref = rest.pop(0)
    nx_ref = rest.pop(0) if emit_next else None
    cast_dst_refs = [rest.pop(0) for _ in range(n_cast)]
    qkv_dst_ref = rest.pop(0) if emit_qkv else None
    z_even_ref, z_odd_ref = rest
    i = pl.program_id(0)
    tm = o_ref.shape[0]

    @pl.when(i == 0)
    def _():
        z_odd_ref[...] = jnp.zeros(z_odd_ref.shape, F32)

    for src_ref, dst_ref in zip(cast_src_refs, cast_dst_refs):
        dst_ref[...] = src_ref[...].astype(dst_ref.dtype)
    if emit_qkv:
        qk_width = ATTN_WIDTH + KV_WIDTH
        qkv_dst_ref[:, :qk_width] = _rotary_layout_block(cast_src_refs[0][:, :qk_width]).astype(qkv_dst_ref.dtype)
        qkv_dst_ref[:, qk_width:] = cast_src_refs[0][:, qk_width:QKV_WIDTH].astype(qkv_dst_ref.dtype)

    def step(z_cur_ref, z_prev_ref):
        z_cur_ref[...] = jnp.dot(a_ref[...], w_ref[...], preferred_element_type=F32)
        for r in range(0, tm, RES_ROWS):
            y = z_prev_ref[r:r + RES_ROWS, :]
            hn = h_ref[r:r + RES_ROWS, :] + res_scale * (y * _rms_scale(y) * post_g_ref[...])
            o_ref[r:r + RES_ROWS, :] = hn
            if emit_next:
                nx_ref[r:r + RES_ROWS, :] = (hn * _rms_scale(hn) * next_g_ref[...]).astype(nx_ref.dtype)

    @pl.when(i % 2 == 0)
    def _():
        step(z_even_ref, z_odd_ref)

    @pl.when(i % 2 == 1)
    def _():
        step(z_odd_ref, z_even_ref)


def _proj_residual(a, w, h, post_g, next_g, res_scale, tm, casts=(), emit_qkv=False):
    t, d = h.shape
    k = a.shape[1]
    n_tiles = t // tm
    emit_next = next_g is not None
    vec = pl.BlockSpec((1, d), lambda i: (0, 0))
    lagged = pl.BlockSpec((tm, d), lambda i: (jnp.maximum(i - 1, 0), 0))
    chunk = lambda i: jnp.minimum(i, n_tiles - 1)
    w_spec = pl.BlockSpec((k, d), lambda i: (0, 0), pipeline_mode=pl.Buffered(1))
    in_specs = [pl.BlockSpec((tm, k), lambda i: (chunk(i), 0)), w_spec, lagged, vec]
    args = [a, w, h, post_g.reshape(1, d)]
    out_specs, out_shape = [lagged], [jax.ShapeDtypeStruct((t, d), F32)]
    if emit_next:
        in_specs.append(vec)
        args.append(next_g.reshape(1, d))
        out_specs.append(lagged)
        out_shape.append(jax.ShapeDtypeStruct((t, d), BF16))
    for src, src_l in casts:
        rows_total, cols = src.shape[1:]
        rows = rows_total // n_tiles
        in_specs.append(pl.BlockSpec((None, rows, cols), lambda i, src_l=src_l: (src_l, chunk(i), 0)))
        args.append(src)
        out_specs.append(pl.BlockSpec((rows, cols), lambda i: (chunk(i), 0)))
        out_shape.append(jax.ShapeDtypeStruct((rows_total, cols), BF16))
    if emit_qkv:
        rows_total = casts[0][0].shape[1]
        out_specs.append(pl.BlockSpec((rows_total // n_tiles, QKV_WIDTH), lambda i: (chunk(i), 0)))
        out_shape.append(jax.ShapeDtypeStruct((rows_total, QKV_WIDTH), BF16))
    outs = pl.pallas_call(
        functools.partial(_proj_residual_kernel, res_scale=res_scale, emit_next=emit_next, n_cast=len(casts),
                          emit_qkv=emit_qkv),
        grid=(n_tiles + 1,),
        in_specs=in_specs, out_specs=out_specs, out_shape=out_shape,
        scratch_shapes=[pltpu.VMEM((tm, d), F32), pltpu.VMEM((tm, d), F32)],
        compiler_params=_params("arbitrary"),
        name="proj_residual",
    )(*args)
    return tuple(outs)


def _qkv_kernel(u_ref, w_ref, cos_ref, sin_ref, o_ref):
    j = pl.program_id(1)
    is_rot = j < (ATTN_WIDTH + KV_WIDTH) // PROJ_TN
    c = jnp.where(is_rot, cos_ref[...], 1.0)
    s = jnp.where(is_rot, sin_ref[...], 0.0)
    u = u_ref[...]
    for n in range(0, PROJ_TN, MXU_COLS):
        z = jnp.dot(u, w_ref[:, n:n + MXU_COLS], preferred_element_type=F32)
        for a in range(0, MXU_COLS, HEAD_DIM):
            x = z[:, a:a + HEAD_DIM]
            partner = pltpu.roll(x, HEAD_DIM // 2, axis=1)
            o_ref[:, n + a:n + a + HEAD_DIM] = (x * c + partner * s).astype(o_ref.dtype)


def _qkv_proj(u, w_qkv, cos_t, sin_t, seq):
    t, d = u.shape
    s_tiles = seq // PROJ_TM
    tab = pl.BlockSpec((PROJ_TM, HEAD_DIM), lambda i, j: (i % s_tiles, 0))
    return pl.pallas_call(
        _qkv_kernel,
        grid=(t // PROJ_TM, QKV_WIDTH // PROJ_TN),
        in_specs=[pl.BlockSpec((PROJ_TM, d), lambda i, j: (i, 0)),
                  _cols(d, PROJ_TN), tab, tab],
        out_specs=pl.BlockSpec((PROJ_TM, PROJ_TN), lambda i, j: (i, j)),
        out_shape=jax.ShapeDtypeStruct((t, QKV_WIDTH), BF16),
        compiler_params=_params("parallel", "arbitrary"),
        name="qkv_proj",
    )(u, w_qkv, cos_t, sin_t)


def _glu_pool_kernel(u_ref, wa_ref, wg_ref, wp_ref, y_ref, p_ref):
    u = u_ref[...]
    a = jnp.dot(u, wa_ref[...], preferred_element_type=F32)
    g = jnp.dot(u, wg_ref[...], preferred_element_type=F32)
    y_ref[...] = (a * jax.nn.sigmoid(g)).astype(y_ref.dtype)
    p_ref[...] = jnp.dot(u, wp_ref[...], preferred_element_type=F32).astype(p_ref.dtype)


def _glu_pool_proj(u, w_in):
    t, d = u.shape
    out = pl.BlockSpec((PROJ_TM, PROJ_TN), lambda i, j: (i, j))
    return pl.pallas_call(
        _glu_pool_kernel,
        grid=(t // PROJ_TM, CONV_WIDTH // PROJ_TN),
        in_specs=[pl.BlockSpec((PROJ_TM, d), lambda i, j: (i, 0)),
                  _cols(d, PROJ_TN, COL_CONV_A),
                  _cols(d, PROJ_TN, COL_CONV_G),
                  _cols(d, PROJ_TN, COL_POOL)],
        out_specs=[out, out],
        out_shape=[jax.ShapeDtypeStruct((t, CONV_WIDTH), BF16),
                   jax.ShapeDtypeStruct((t, POOL_WIDTH), BF16)],
        compiler_params=_params("parallel", "arbitrary"),
        name="glu_pool_proj",
    )(u, w_in, w_in, w_in)


def _attn_kernel(sink_ref, q_ref, kp_ref, km_ref, kn_ref, vp_ref, vm_ref, vn_ref, o_ref,
                 kx_ref, vx_ref, wb_ref):
    i = pl.program_id(1)
    n_sub = ATT_QB // BLOCK
    seq = pl.num_programs(1) * ATT_QB
    keys = 3 * BLOCK

    kx_ref[0:BLOCK, :] = kp_ref[...]
    kx_ref[BLOCK:BLOCK + ATT_QB, :] = km_ref[...]
    kx_ref[BLOCK + ATT_QB:, :] = kn_ref[...]
    vx_ref[0:BLOCK, :] = vp_ref[...]
    vx_ref[BLOCK:BLOCK + ATT_QB, :] = vm_ref[...]
    vx_ref[BLOCK + ATT_QB:, :] = vn_ref[...]

    qi = lax.broadcasted_iota(jnp.int32, (BLOCK, keys), 0)
    kj = lax.broadcasted_iota(jnp.int32, (BLOCK, keys), 1)
    wb_ref[...] = jnp.where(jnp.abs(kj - BLOCK - qi) <= WINDOW, 0.0, NEG_INF).astype(F32)

    logit_scale = (HEAD_DIM ** -0.5) * LOG2_E

    def sub_block(j, carry):
        r0 = pl.multiple_of(j * BLOCK, BLOCK)
        key_pos = (i * n_sub + j) * BLOCK - BLOCK + lax.broadcasted_iota(jnp.int32, (1, keys), 1)
        bias = wb_ref[...] + jnp.where((key_pos >= 0) & (key_pos < seq), 0.0, NEG_INF).astype(F32)
        for hh in range(N_KV_HEADS):
            kc = kx_ref[pl.ds(r0, keys), hh * HEAD_DIM:(hh + 1) * HEAD_DIM]
            vc = vx_ref[pl.ds(r0, keys), hh * HEAD_DIM:(hh + 1) * HEAD_DIM]
            qs = jnp.concatenate(
                [q_ref[pl.ds(r0, BLOCK), (hh * Q_PER_KV + g) * HEAD_DIM:(hh * Q_PER_KV + g + 1) * HEAD_DIM]
                 for g in range(Q_PER_KV)], axis=0)
            s = lax.dot_general(qs, kc, (((1,), (1,)), ((), ())), preferred_element_type=F32)
            probs, inv = [], []
            for g in range(Q_PER_KV):
                sk = sink_ref[hh * Q_PER_KV + g] * LOG2_E
                lg = s[g * BLOCK:(g + 1) * BLOCK, :] * logit_scale + bias
                m = jnp.maximum(jnp.max(lg, axis=-1, keepdims=True), sk)
                p = jnp.exp2(lg - m)
                inv.append(1.0 / (jnp.sum(p, axis=-1, keepdims=True) + jnp.exp2(sk - m)))
                probs.append(p.astype(BF16))
            pv = jnp.dot(jnp.concatenate(probs, axis=0), vc, preferred_element_type=F32)
            for g in range(Q_PER_KV):
                c0 = (hh * Q_PER_KV + g) * HEAD_DIM
                o_ref[pl.ds(r0, BLOCK), c0:c0 + HEAD_DIM] = (
                    pv[g * BLOCK:(g + 1) * BLOCK, :] * inv[g]).astype(o_ref.dtype)
        return carry

    lax.fori_loop(0, n_sub, sub_block, 0)


def _windowed_attention(qkv, sink_l, batch, seq):
    qkv3 = qkv.reshape(batch, seq, QKV_WIDTH)
    n_sub = ATT_QB // BLOCK
    n_blk = seq // BLOCK
    k_col = ATTN_WIDTH // KV_WIDTH
    v_col = k_col + 1

    def halo(col):
        prev = pl.BlockSpec((None, BLOCK, KV_WIDTH), lambda b, i: (b, jnp.maximum(i * n_sub - 1, 0), col))
        main = pl.BlockSpec((None, ATT_QB, KV_WIDTH), lambda b, i: (b, i, col))
        nxt = pl.BlockSpec((None, BLOCK, KV_WIDTH), lambda b, i: (b, jnp.minimum((i + 1) * n_sub, n_blk - 1), col))
        return [prev, main, nxt]

    out = pl.pallas_call(
        _attn_kernel,
        grid=(batch, seq // ATT_QB),
        in_specs=[pl.BlockSpec(memory_space=pltpu.SMEM),
                  pl.BlockSpec((None, ATT_QB, ATTN_WIDTH), lambda b, i: (b, i, 0))] + halo(k_col) + halo(v_col),
        out_specs=pl.BlockSpec((None, ATT_QB, ATTN_WIDTH), lambda b, i: (b, i, 0)),
        out_shape=jax.ShapeDtypeStruct((batch, seq, ATTN_WIDTH), BF16),
        scratch_shapes=[pltpu.VMEM((ATT_QB + 2 * BLOCK, KV_WIDTH), BF16),
                        pltpu.VMEM((ATT_QB + 2 * BLOCK, KV_WIDTH), BF16),
                        pltpu.VMEM((BLOCK, 3 * BLOCK), F32)],
        compiler_params=_params("parallel", "arbitrary"),
        name="window_attn",
    )(sink_l, qkv3, qkv3, qkv3, qkv3, qkv3, qkv3, qkv3)
    return out.reshape(batch * seq, ATTN_WIDTH)


def _conv_pool_kernel(yp_ref, ym_ref, yn_ref, pp_ref, pm_ref, pn_ref, wdw_ref, bdw_ref, lng_ref, lnb_ref,
                      band_ref, pw_ref, ps_ref, co_ref, po_ref, yx_ref, px_ref, sh_ref, cb_ref):
    i = pl.program_id(1)
    n_i = pl.num_programs(1)
    ts = ym_ref.shape[0]
    seq = n_i * ts

    has_prev = i > 0
    has_next = i < n_i - 1
    yx_ref[0:HALO, :] = jnp.where(has_prev, yp_ref[...], 0).astype(F32)
    yx_ref[HALO:HALO + ts, :] = ym_ref[...].astype(F32)
    yx_ref[HALO + ts:, :] = jnp.where(has_next, yn_ref[...], 0).astype(F32)
    px_ref[0:HALO, :] = jnp.where(has_prev, pp_ref[...], 0)
    px_ref[HALO:HALO + ts, :] = pm_ref[...]
    px_ref[HALO + ts:, :] = jnp.where(has_next, pn_ref[...], 0)

    n_sh = sh_ref.shape[1]
    for b in range(1, SUBLANES):
        sh_ref[b - 1] = yx_ref[b:b + n_sh, :]

    for r in range(0, ts, CP_ROWS):
        for c in range(0, CONV_WIDTH, LANES):
            acc = jnp.zeros((CP_ROWS, LANES), F32)
            for k in range(CONV_SIZE):
                a, b = divmod(k + HALO - CONV_PAD, SUBLANES)
                src = yx_ref if b == 0 else sh_ref.at[b - 1]
                r_src = r + a * SUBLANES
                acc = acc + src[r_src:r_src + CP_ROWS, c:c + LANES] * wdw_ref[k:k + 1, c:c + LANES]
            cb_ref[r:r + CP_ROWS, c:c + LANES] = acc + bdw_ref[:, c:c + LANES]

    x = cb_ref[...]
    mu = jnp.sum(x, axis=-1, keepdims=True) * (1.0 / CONV_WIDTH)
    xc = x - mu
    var = jnp.sum(xc * xc, axis=-1, keepdims=True) * (1.0 / CONV_WIDTH)
    yn = xc * lax.rsqrt(var + LN_EPS) * lng_ref[...] + lnb_ref[...]
    co_ref[...] = (yn * jax.nn.sigmoid(yn)).astype(co_ref.dtype)

    pos = i * ts + lax.broadcasted_iota(jnp.int32, (ts, 1), 0)
    for gi, size in enumerate(POOL_SIZES):
        half = size // 2
        c0 = gi * POOL_GROUP
        tot = jnp.dot(band_ref[gi], px_ref[:, c0:c0 + POOL_GROUP], preferred_element_type=F32)
        lo = jnp.clip(pos - half, 0, seq - 1)
        hi = jnp.clip(pos + half - 1, 0, seq - 1)
        cnt = (hi - lo + 1).astype(F32)
        mixed = (tot / cnt - pm_ref[:, c0:c0 + POOL_GROUP].astype(F32)).astype(BF16)
        yg = jnp.dot(mixed, pw_ref[gi], preferred_element_type=F32)
        po_ref[:, c0:c0 + POOL_GROUP] = (yg * ps_ref[:, c0:c0 + POOL_GROUP]).astype(po_ref.dtype)


def _pool_bands(ts):
    t = jnp.arange(ts)[:, None] + HALO
    r = jnp.arange(ts + 2 * HALO)[None, :]
    return jnp.stack([((r >= t - size // 2) & (r <= t + size // 2 - 1)).astype(BF16) for size in POOL_SIZES])


def _conv_pool(y, p, w_dw, b_dw, ln_g, ln_b, pool_w, pool_scale, l, batch, seq):
    y3 = y.reshape(batch, seq, CONV_WIDTH)
    p3 = p.reshape(batch, seq, POOL_WIDTH)
    per = CP_TS // HALO
    n_halo = seq // HALO
    ext = CP_TS + 2 * HALO

    def halo(width):
        prev = pl.BlockSpec((None, HALO, width), lambda b, i: (b, jnp.maximum(i * per - 1, 0), 0))
        main = pl.BlockSpec((None, CP_TS, width), lambda b, i: (b, i, 0))
        nxt = pl.BlockSpec((None, HALO, width), lambda b, i: (b, jnp.minimum((i + 1) * per, n_halo - 1), 0))
        return [prev, main, nxt]

    def layer(*shape):
        return pl.BlockSpec((None,) + shape, lambda b, i: (l,) + (0,) * len(shape))

    conv, pool = pl.pallas_call(
        _conv_pool_kernel,
        grid=(batch, seq // CP_TS),
        in_specs=halo(CONV_WIDTH) + halo(POOL_WIDTH) + [
            layer(CONV_SIZE, CONV_WIDTH), layer(1, CONV_WIDTH), layer(1, CONV_WIDTH), layer(1, CONV_WIDTH),
            pl.BlockSpec((N_POOL_GROUPS, CP_TS, ext), lambda b, i: (0, 0, 0)),
            layer(N_POOL_GROUPS, POOL_GROUP, POOL_GROUP), layer(1, POOL_WIDTH)],
        out_specs=[pl.BlockSpec((None, CP_TS, CONV_WIDTH), lambda b, i: (b, i, 0)),
                   pl.BlockSpec((None, CP_TS, POOL_WIDTH), lambda b, i: (b, i, 0))],
        out_shape=[jax.ShapeDtypeStruct((batch, seq, CONV_WIDTH), BF16),
                   jax.ShapeDtypeStruct((batch, seq, POOL_WIDTH), BF16)],
        scratch_shapes=[pltpu.VMEM((ext, CONV_WIDTH), F32),
                        pltpu.VMEM((ext, POOL_WIDTH), BF16),
                        pltpu.VMEM((SUBLANES - 1, ext - SUBLANES, CONV_WIDTH), F32),
                        pltpu.VMEM((CP_TS, CONV_WIDTH), F32)],
        compiler_params=_params("parallel", "arbitrary"),
        name="conv_pool",
    )(y3, y3, y3, p3, p3, p3, w_dw, b_dw[:, None, :], ln_g[:, None, :], ln_b[:, None, :],
      _pool_bands(CP_TS), pool_w, pool_scale[:, None, :])
    return conv.reshape(batch * seq, CONV_WIDTH), pool.reshape(batch * seq, POOL_WIDTH)


def _gated_merge_kernel(u_ref, at_ref, cv_ref, pl_ref, wg0_ref, wg1_ref, wg2_ref, wa_ref, wc_ref, wp_ref, o_ref):
    u = u_ref[...]

    def gate(w_ref):
        return jax.nn.sigmoid(jnp.dot(u, w_ref[...], preferred_element_type=F32))

    m = gate(wg0_ref) * jnp.dot(at_ref[...], wa_ref[...], preferred_element_type=F32)
    m = m + gate(wg1_ref) * jnp.dot(cv_ref[...], wc_ref[...], preferred_element_type=F32)
    m = m + gate(wg2_ref) * jnp.dot(pl_ref[...], wp_ref[...], preferred_element_type=F32)
    o_ref[...] = m.astype(o_ref.dtype)


def _gated_merge(u, attn, conv, pool, w_in, w_attn, w_conv, w_pool):
    t, d = u.shape
    row = lambda width: pl.BlockSpec((MERGE_TM, width), lambda i, n: (i, 0))
    return pl.pallas_call(
        _gated_merge_kernel,
        grid=(t // MERGE_TM, d // MERGE_TN),
        in_specs=[row(d), row(ATTN_WIDTH), row(CONV_WIDTH), row(POOL_WIDTH)]
        + [_cols(d, MERGE_TN, COL_GATES + b * d) for b in range(N_BRANCHES)]
        + [_cols(ATTN_WIDTH, MERGE_TN), _cols(CONV_WIDTH, MERGE_TN), _cols(POOL_WIDTH, MERGE_TN)],
        out_specs=pl.BlockSpec((MERGE_TM, MERGE_TN), lambda i, n: (i, n)),
        out_shape=jax.ShapeDtypeStruct((t, d), BF16),
        compiler_params=_params("parallel", "arbitrary"),
        name="gated_merge",
    )(u, attn, conv, pool, w_in, w_in, w_in, w_attn, w_conv, w_pool)


def _rope_lane_tables(seq):
    pos = jnp.arange(seq, dtype=F32)
    inv_freq = ROPE_THETA ** (-jnp.arange(0, ROT_DIM, 2, dtype=F32) / ROT_DIM)
    ang = pos[:, None] * inv_freq[None, :]
    cos, sin = jnp.cos(ang), jnp.sin(ang)
    ones = jnp.ones((seq, HEAD_DIM // 2 - ROT_HALF), F32)
    zeros = jnp.zeros((seq, HEAD_DIM // 2 - ROT_HALF), F32)
    cos_t = jnp.concatenate([cos, ones, cos, ones], axis=-1)
    sin_t = jnp.concatenate([-sin, zeros, sin, zeros], axis=-1)
    return cos_t, sin_t


def kernel(x, ffn1_pre_g, ffn1_post_g, ffn1_w_gate, ffn1_w_up, ffn1_w_down, mix_pre_g, mix_post_g, w_in, sink,
           w_attn_proj, conv_dw, conv_dw_b, conv_ln_g, conv_ln_b, w_conv_proj, pool_w, pool_scale, w_pool_proj,
           w_out, ffn2_pre_g, ffn2_post_g, ffn2_w_gate, ffn2_w_up, ffn2_w_down):
    batch, seq, d = x.shape
    cos_t, sin_t = _rope_lane_tables(seq)
    pool_w_b = pool_w.astype(BF16)

    h = x.reshape(batch * seq, d)
    xn = _rmsnorm_bf16(h, ffn1_pre_g[0])
    for l in range(DEPTH):
        act, w_down_b = _gate_up(xn, ffn1_w_gate, ffn1_w_up, ffn1_w_down, l)
        h, u, w_in_b, w_attn_b, w_conv_b, w_pool_b, w_out_b, w_qkv_b = _proj_residual(
            act, w_down_b, h, ffn1_post_g[l], mix_pre_g[l], 0.5, DOWN_TM,
            casts=((w_in, l), (w_attn_proj, l), (w_conv_proj, l), (w_pool_proj, l), (w_out, l)), emit_qkv=True)
        qkv = _qkv_proj(u, w_qkv_b, cos_t, sin_t, seq)
        y, p = _glu_pool_proj(u, w_in_b)
        attn = _windowed_attention(qkv, sink[l], batch, seq)
        conv, pool = _conv_pool(y, p, conv_dw, conv_dw_b, conv_ln_g, conv_ln_b, pool_w_b, pool_scale, l, batch, seq)
        merged = _gated_merge(u, attn, conv, pool, w_in_b, w_attn_b, w_conv_b, w_pool_b)
        h, xn = _proj_residual(merged, w_out_b, h, mix_post_g[l], ffn2_pre_g[l], 1.0, OUT_TM)
        act, w_down_b = _gate_up(xn, ffn2_w_gate, ffn2_w_up, ffn2_w_down, l)
        if l + 1 < DEPTH:
            h, xn = _proj_residual(act, w_down_b, h, ffn2_post_g[l], ffn1_pre_g[l + 1], 0.5, DOWN_TM)
        else:
            h, = _proj_residual(act, w_down_b, h, ffn2_post_g[l], None, 0.5, DOWN_TM)
    return h.reshape(batch, seq, d)
```

```python
import functools
import math

import jax
import jax.numpy as jnp
from jax import lax
from jax.experimental import pallas as pl
from jax.experimental.pallas import tpu as pltpu

D_MODEL = 2048
DEPTH = 4
HEAD_DIM = 128
N_Q_HEADS = D_MODEL // HEAD_DIM
N_KV_HEADS = N_Q_HEADS // 4
Q_PER_KV = N_Q_HEADS // N_KV_HEADS
ATTN_WIDTH = N_Q_HEADS * HEAD_DIM
KV_WIDTH = N_KV_HEADS * HEAD_DIM
WINDOW = 128
BLOCK = 128
ROPE_THETA = 500000.0
ROT_DIM = HEAD_DIM // 4
ROT_HALF = ROT_DIM // 2
CONV_WIDTH = D_MODEL // 2
CONV_SIZE = 31
CONV_PAD = CONV_SIZE // 2
POOL_WIDTH = D_MODEL // 2
POOL_SIZES = (2, 4, 8, 16)
N_POOL_GROUPS = len(POOL_SIZES)
POOL_GROUP = POOL_WIDTH // N_POOL_GROUPS
N_BRANCHES = 3
RMS_EPS = 1e-6
LN_EPS = 1e-5
NEG_INF = -1e30
LOG2_E = math.log2(math.e)
LOGIT_SCALE_LOG2 = (HEAD_DIM ** -0.5) * LOG2_E

QKV_WIDTH = ATTN_WIDTH + 2 * KV_WIDTH
COL_CONV_A = QKV_WIDTH
COL_CONV_G = COL_CONV_A + CONV_WIDTH
COL_POOL = COL_CONV_G + CONV_WIDTH
COL_GATES = COL_POOL + POOL_WIDTH

LANES = 128
SUBLANES = 8
MXU_COLS = 256
VMEM_LIMIT_BYTES = 56 * 1024 * 1024

NORM_ROWS = 512
UP_TM = 1024
UP_TF = 512
DOWN_TM = 256
OUT_TM = 512
RES_ROWS = 128
PROJ_TM = 1024
PROJ_TN = 512
ATT_QB = 1024
HALO = 16
CP_TS = 256
CP_ROWS = 64
MERGE_TM = 1024
MERGE_TN = 256

F32 = jnp.float32
BF16 = jnp.bfloat16


def _params(*semantics):
    return pltpu.CompilerParams(dimension_semantics=semantics, vmem_limit_bytes=VMEM_LIMIT_BYTES)


def _rms_scale(x):
    return lax.rsqrt(jnp.sum(x * x, axis=-1, keepdims=True) * (1.0 / x.shape[-1]) + RMS_EPS)


def _cols(k, tn, col0=0):
    return pl.BlockSpec((k, tn), lambda i, j: (0, col0 // tn + j))


def _cast_specs(casts, n_steps, chunk):
    in_specs, args, out_specs, out_shape = [], [], [], []
    for src, src_l in casts:
        rows_total, cols = src.shape[1:]
        rows = rows_total // n_steps
        in_specs.append(pl.BlockSpec((None, rows, cols), lambda *g, src_l=src_l: (src_l, chunk(*g), 0)))
        args.append(src)
        out_specs.append(pl.BlockSpec((rows, cols), lambda *g: (chunk(*g), 0)))
        out_shape.append(jax.ShapeDtypeStruct((rows_total, cols), BF16))
    return in_specs, args, out_specs, out_shape


def _rmsnorm_kernel(h_ref, g_ref, *rest):
    n_cast = (len(rest) - 1) // 2
    o_ref = rest[n_cast]
    x = h_ref[...]
    o_ref[...] = (x * _rms_scale(x) * g_ref[...]).astype(o_ref.dtype)
    for src_ref, dst_ref in zip(rest[:n_cast], rest[n_cast + 1:]):
        dst_ref[...] = src_ref[...].astype(dst_ref.dtype)


def _rmsnorm_bf16(h, g, casts=()):
    t, d = h.shape
    n_steps = t // NORM_ROWS
    c_in, c_args, c_out, c_shape = _cast_specs(casts, n_steps, lambda i: i)
    return pl.pallas_call(
        _rmsnorm_kernel,
        grid=(n_steps,),
        in_specs=[pl.BlockSpec((NORM_ROWS, d), lambda i: (i, 0)),
                  pl.BlockSpec((1, d), lambda i: (0, 0))] + c_in,
        out_specs=[pl.BlockSpec((NORM_ROWS, d), lambda i: (i, 0))] + c_out,
        out_shape=[jax.ShapeDtypeStruct((t, d), BF16)] + c_shape,
        compiler_params=_params("parallel"),
        name="rmsnorm",
    )(h, g.reshape(1, d), *c_args)


def _gate_up_kernel(x_ref, wg_ref, wu_ref, wd_ref, o_ref, wd_bf_ref):
    x = x_ref[...]
    g = jnp.dot(x, wg_ref[...], preferred_element_type=F32)
    u = jnp.dot(x, wu_ref[...], preferred_element_type=F32)
    o_ref[...] = (g * jax.nn.sigmoid(g) * u).astype(o_ref.dtype)
    wd_bf_ref[...] = wd_ref[...].astype(wd_bf_ref.dtype)


def _gate_up(xn, w_gate, w_up, w_down, l):
    t, d = xn.shape
    ff = w_gate.shape[1]
    n_j = ff // UP_TF
    c_in, c_args, c_out, c_shape = _cast_specs(((w_down, l),), (t // UP_TM) * n_j, lambda i, j: i * n_j + j)
    return pl.pallas_call(
        _gate_up_kernel,
        grid=(t // UP_TM, n_j),
        in_specs=[pl.BlockSpec((UP_TM, d), lambda i, j: (i, 0)), _cols(d, UP_TF), _cols(d, UP_TF)] + c_in,
        out_specs=[pl.BlockSpec((UP_TM, UP_TF), lambda i, j: (i, j))] + c_out,
        out_shape=[jax.ShapeDtypeStruct((t, ff), BF16)] + c_shape,
        compiler_params=_params("parallel", "arbitrary"),
        name="gate_up",
    )(xn, w_gate, w_up, *c_args)


def _rotary_layout_block(x):
    lane = lax.broadcasted_iota(jnp.int32, (x.shape[0], HEAD_DIM), 1)
    from_above = (lane >= ROT_HALF) & (lane < HEAD_DIM // 2)
    from_below = (lane >= HEAD_DIM // 2) & (lane < HEAD_DIM // 2 + ROT_HALF)
    heads = []
    for c in range(0, x.shape[1], HEAD_DIM):
        xh = x[:, c:c + HEAD_DIM]
        up = pltpu.roll(xh, HEAD_DIM - ROT_HALF, axis=1)
        down = pltpu.roll(xh, HEAD_DIM // 2 - ROT_HALF, axis=1)
        heads.append(jnp.where(from_above, up, jnp.where(from_below, down, xh)))
    return jnp.concatenate(heads, axis=1)


def _proj_residual_kernel(a_ref, w_ref, h_ref, post_g_ref, *rest, res_scale, emit_next, n_cast, emit_qkv):
    rest = list(rest)
    next_g_ref = rest.pop(0) if emit_next else None
    cast_src_refs = [rest.pop(0) for _ in range(n_cast)]
    o_ref = rest.pop(0)
    nx_ref = rest.pop(0) if emit_next else None
    cast_dst_refs = [rest.pop(0) for _ in range(n_cast)]
    qkv_dst_ref = rest.pop(0) if emit_qkv else None
    z_even_ref, z_odd_ref = rest
    i = pl.program_id(0)
    tm = o_ref.shape[0]

    @pl.when(i == 0)
    def _():
        z_odd_ref[...] = jnp.zeros(z_odd_ref.shape, F32)

    for src_ref, dst_ref in zip(cast_src_refs, cast_dst_refs):
        dst_ref[...] = src_ref[...].astype(dst_ref.dtype)
    if emit_qkv:
        qk_width = ATTN_WIDTH + KV_WIDTH
        qkv_dst_ref[:, :qk_width] = _rotary_layout_block(cast_src_refs[0][:, :qk_width]).astype(qkv_dst_ref.dtype)
        qkv_dst_ref[:, qk_width:] = cast_src_refs[0][:, qk_width:QKV_WIDTH].astype(qkv_dst_ref.dtype)

    def step(z_cur_ref, z_prev_ref):
        z_cur_ref[...] = jnp.dot(a_ref[...], w_ref[...], preferred_element_type=F32)
        for r in range(0, tm, RES_ROWS):
            y = z_prev_ref[r:r + RES_ROWS, :]
            hn = h_ref[r:r + RES_ROWS, :] + res_scale * (y * _rms_scale(y) * post_g_ref[...])
            o_ref[r:r + RES_ROWS, :] = hn
            if emit_next:
                nx_ref[r:r + RES_ROWS, :] = (hn * _rms_scale(hn) * next_g_ref[...]).astype(nx_ref.dtype)

    @pl.when(i % 2 == 0)
    def _():
        step(z_even_ref, z_odd_ref)

    @pl.when(i % 2 == 1)
    def _():
        step(z_odd_ref, z_even_ref)


def _proj_residual(a, w, h, post_g, next_g, res_scale, tm, casts=(), emit_qkv=False):
    t, d = h.shape
    k = a.shape[1]
    n_tiles = t // tm
    emit_next = next_g is not None
    vec = pl.BlockSpec((1, d), lambda i: (0, 0))
    lagged = pl.BlockSpec((tm, d), lambda i: (jnp.maximum(i - 1, 0), 0))
    chunk = lambda i: jnp.minimum(i, n_tiles - 1)
    w_spec = pl.BlockSpec((k, d), lambda i: (0, 0), pipeline_mode=pl.Buffered(1))
    in_specs = [pl.BlockSpec((tm, k), lambda i: (chunk(i), 0)), w_spec, lagged, vec]
    args = [a, w, h, post_g.reshape(1, d)]
    out_specs, out_shape = [lagged], [jax.ShapeDtypeStruct((t, d), F32)]
    if emit_next:
        in_specs.append(vec)
        args.append(next_g.reshape(1, d))
        out_specs.append(lagged)
        out_shape.append(jax.ShapeDtypeStruct((t, d), BF16))
    c_in, c_args, c_out, c_shape = _cast_specs(casts, n_tiles, chunk)
    in_specs += c_in
    args += c_args
    out_specs += c_out
    out_shape += c_shape
    if emit_qkv:
        rows_total = casts[0][0].shape[1]
        out_specs.append(pl.BlockSpec((rows_total // n_tiles, QKV_WIDTH), lambda i: (chunk(i), 0)))
        out_shape.append(jax.ShapeDtypeStruct((rows_total, QKV_WIDTH), BF16))
    outs = pl.pallas_call(
        functools.partial(_proj_residual_kernel, res_scale=res_scale, emit_next=emit_next, n_cast=len(casts),
                          emit_qkv=emit_qkv),
        grid=(n_tiles + 1,),
        in_specs=in_specs, out_specs=out_specs, out_shape=out_shape,
        scratch_shapes=[pltpu.VMEM((tm, d), F32), pltpu.VMEM((tm, d), F32)],
        compiler_params=_params("arbitrary"),
        name="proj_residual",
    )(*args)
    return tuple(outs)


def _qkv_kernel(u_ref, w_ref, cos_ref, sin_ref, o_ref):
    j = pl.program_id(1)
    is_rot = j < (ATTN_WIDTH + KV_WIDTH) // PROJ_TN
    c = jnp.where(is_rot, cos_ref[...], 1.0)
    s = jnp.where(is_rot, sin_ref[...], 0.0)
    u = u_ref[...]
    for n in range(0, PROJ_TN, MXU_COLS):
        z = jnp.dot(u, w_ref[:, n:n + MXU_COLS], preferred_element_type=F32)
        for a in range(0, MXU_COLS, HEAD_DIM):
            x = z[:, a:a + HEAD_DIM]
            partner = pltpu.roll(x, HEAD_DIM // 2, axis=1)
            o_ref[:, n + a:n + a + HEAD_DIM] = (x * c + partner * s).astype(o_ref.dtype)


def _qkv_proj(u, w_qkv, cos_t, sin_t, seq):
    t, d = u.shape
    s_tiles = seq // PROJ_TM
    tab = pl.BlockSpec((PROJ_TM, HEAD_DIM), lambda i, j: (i % s_tiles, 0))
    return pl.pallas_call(
        _qkv_kernel,
        grid=(t // PROJ_TM, QKV_WIDTH // PROJ_TN),
        in_specs=[pl.BlockSpec((PROJ_TM, d), lambda i, j: (i, 0)),
                  _cols(d, PROJ_TN), tab, tab],
        out_specs=pl.BlockSpec((PROJ_TM, PROJ_TN), lambda i, j: (i, j)),
        out_shape=jax.ShapeDtypeStruct((t, QKV_WIDTH), BF16),
        compiler_params=_params("parallel", "arbitrary"),
        name="qkv_proj",
    )(u, w_qkv, cos_t, sin_t)


def _glu_pool_kernel(u_ref, wa_ref, wg_ref, wp_ref, y_ref, p_ref):
    u = u_ref[...]
    a = jnp.dot(u, wa_ref[...], preferred_element_type=F32)
    g = jnp.dot(u, wg_ref[...], preferred_element_type=F32)
    y_ref[...] = (a * jax.nn.sigmoid(g)).astype(y_ref.dtype)
    p_ref[...] = jnp.dot(u, wp_ref[...], preferred_element_type=F32).astype(p_ref.dtype)


def _glu_pool_proj(u, w_in):
    t, d = u.shape
    out = pl.BlockSpec((PROJ_TM, PROJ_TN), lambda i, j: (i, j))
    return pl.pallas_call(
        _glu_pool_kernel,
        grid=(t // PROJ_TM, CONV_WIDTH // PROJ_TN),
        in_specs=[pl.BlockSpec((PROJ_TM, d), lambda i, j: (i, 0)),
                  _cols(d, PROJ_TN, COL_CONV_A),
                  _cols(d, PROJ_TN, COL_CONV_G),
                  _cols(d, PROJ_TN, COL_POOL)],
        out_specs=[out, out],
        out_shape=[jax.ShapeDtypeStruct((t, CONV_WIDTH), BF16),
                   jax.ShapeDtypeStruct((t, POOL_WIDTH), BF16)],
        compiler_params=_params("parallel", "arbitrary"),
        name="glu_pool_proj",
    )(u, w_in, w_in, w_in)


def _attn_kernel(sink_ref, q_ref, kp_ref, km_ref, kn_ref, vp_ref, vm_ref, vn_ref, o_ref,
                 kx_ref, vx_ref, wb_ref):
    i = pl.program_id(1)
    n_sub = ATT_QB // BLOCK
    seq = pl.num_programs(1) * ATT_QB
    keys = 3 * BLOCK

    kx_ref[0:BLOCK, :] = kp_ref[...]
    kx_ref[BLOCK:BLOCK + ATT_QB, :] = km_ref[...]
    kx_ref[BLOCK + ATT_QB:, :] = kn_ref[...]
    vx_ref[0:BLOCK, :] = vp_ref[...]
    vx_ref[BLOCK:BLOCK + ATT_QB, :] = vm_ref[...]
    vx_ref[BLOCK + ATT_QB:, :] = vn_ref[...]

    qi = lax.broadcasted_iota(jnp.int32, (BLOCK, keys), 0)
    kj = lax.broadcasted_iota(jnp.int32, (BLOCK, keys), 1)
    wb_ref[...] = jnp.where(jnp.abs(kj - BLOCK - qi) <= WINDOW, 0.0, NEG_INF).astype(F32)

    def sub_block(j, carry):
        r0 = pl.multiple_of(j * BLOCK, BLOCK)
        key_pos = (i * n_sub + j) * BLOCK - BLOCK + lax.broadcasted_iota(jnp.int32, (1, keys), 1)
        bias = wb_ref[...] + jnp.where((key_pos >= 0) & (key_pos < seq), 0.0, NEG_INF).astype(F32)
        for hh in range(N_KV_HEADS):
            kc = kx_ref[pl.ds(r0, keys), hh * HEAD_DIM:(hh + 1) * HEAD_DIM]
            vc = vx_ref[pl.ds(r0, keys), hh * HEAD_DIM:(hh + 1) * HEAD_DIM]
            qs = jnp.concatenate(
                [q_ref[pl.ds(r0, BLOCK), (hh * Q_PER_KV + g) * HEAD_DIM:(hh * Q_PER_KV + g + 1) * HEAD_DIM]
                 for g in range(Q_PER_KV)], axis=0)
            s = lax.dot_general(qs, kc, (((1,), (1,)), ((), ())), preferred_element_type=F32)
            probs, inv = [], []
            for g in range(Q_PER_KV):
                sk = sink_ref[hh * Q_PER_KV + g] * LOG2_E
                lg = s[g * BLOCK:(g + 1) * BLOCK, :] * LOGIT_SCALE_LOG2 + bias
                m = jnp.maximum(jnp.max(lg, axis=-1, keepdims=True), sk)
                p = jnp.exp2(lg - m)
                inv.append(1.0 / (jnp.sum(p, axis=-1, keepdims=True) + jnp.exp2(sk - m)))
                probs.append(p.astype(BF16))
            pv = jnp.dot(jnp.concatenate(probs, axis=0), vc, preferred_element_type=F32)
            for g in range(Q_PER_KV):
                c0 = (hh * Q_PER_KV + g) * HEAD_DIM
                o_ref[pl.ds(r0, BLOCK), c0:c0 + HEAD_DIM] = (
                    pv[g * BLOCK:(g + 1) * BLOCK, :] * inv[g]).astype(o_ref.dtype)
        return carry

    lax.fori_loop(0, n_sub, sub_block, 0)


def _windowed_attention(qkv, sink_l, batch, seq):
    qkv3 = qkv.reshape(batch, seq, QKV_WIDTH)
    n_sub = ATT_QB // BLOCK
    n_blk = seq // BLOCK
    k_col = ATTN_WIDTH // KV_WIDTH
    v_col = k_col + 1

    def halo(col):
        prev = pl.BlockSpec((None, BLOCK, KV_WIDTH), lambda b, i: (b, jnp.maximum(i * n_sub - 1, 0), col))
        main = pl.BlockSpec((None, ATT_QB, KV_WIDTH), lambda b, i: (b, i, col))
        nxt = pl.BlockSpec((None, BLOCK, KV_WIDTH), lambda b, i: (b, jnp.minimum((i + 1) * n_sub, n_blk - 1), col))
        return [prev, main, nxt]

    out = pl.pallas_call(
        _attn_kernel,
        grid=(batch, seq // ATT_QB),
        in_specs=[pl.BlockSpec(memory_space=pltpu.SMEM),
                  pl.BlockSpec((None, ATT_QB, ATTN_WIDTH), lambda b, i: (b, i, 0))] + halo(k_col) + halo(v_col),
        out_specs=pl.BlockSpec((None, ATT_QB, ATTN_WIDTH), lambda b, i: (b, i, 0)),
        out_shape=jax.ShapeDtypeStruct((batch, seq, ATTN_WIDTH), BF16),
        scratch_shapes=[pltpu.VMEM((ATT_QB + 2 * BLOCK, KV_WIDTH), BF16),
                        pltpu.VMEM((ATT_QB + 2 * BLOCK, KV_WIDTH), BF16),
                        pltpu.VMEM((BLOCK, 3 * BLOCK), F32)],
        compiler_params=_params("parallel", "arbitrary"),
        name="window_attn",
    )(sink_l, qkv3, qkv3, qkv3, qkv3, qkv3, qkv3, qkv3)
    return out.reshape(batch * seq, ATTN_WIDTH)


def _conv_pool_kernel(yp_ref, ym_ref, yn_ref, pp_ref, pm_ref, pn_ref, wdw_ref, bdw_ref, lng_ref, lnb_ref,
                      band_ref, pw_ref, ps_ref, co_ref, po_ref, yx_ref, px_ref, sh_ref, cb_ref):
    i = pl.program_id(1)
    n_i = pl.num_programs(1)
    ts = ym_ref.shape[0]
    seq = n_i * ts

    has_prev = i > 0
    has_next = i < n_i - 1
    yx_ref[0:HALO, :] = jnp.where(has_prev, yp_ref[...], 0).astype(F32)
    yx_ref[HALO:HALO + ts, :] = ym_ref[...].astype(F32)
    yx_ref[HALO + ts:, :] = jnp.where(has_next, yn_ref[...], 0).astype(F32)
    px_ref[0:HALO, :] = jnp.where(has_prev, pp_ref[...], 0)
    px_ref[HALO:HALO + ts, :] = pm_ref[...]
    px_ref[HALO + ts:, :] = jnp.where(has_next, pn_ref[...], 0)

    n_sh = sh_ref.shape[1]
    for b in range(1, SUBLANES):
        sh_ref[b - 1] = yx_ref[b:b + n_sh, :]

    for r in range(0, ts, CP_ROWS):
        for c in range(0, CONV_WIDTH, LANES):
            acc = jnp.zeros((CP_ROWS, LANES), F32)
            for k in range(CONV_SIZE):
                a, b = divmod(k + HALO - CONV_PAD, SUBLANES)
                src = yx_ref if b == 0 else sh_ref.at[b - 1]
                r_src = r + a * SUBLANES
                acc = acc + src[r_src:r_src + CP_ROWS, c:c + LANES] * wdw_ref[k:k + 1, c:c + LANES]
            cb_ref[r:r + CP_ROWS, c:c + LANES] = acc + bdw_ref[:, c:c + LANES]

    x = cb_ref[...]
    mu = jnp.sum(x, axis=-1, keepdims=True) * (1.0 / CONV_WIDTH)
    xc = x - mu
    var = jnp.sum(xc * xc, axis=-1, keepdims=True) * (1.0 / CONV_WIDTH)
    yn = xc * lax.rsqrt(var + LN_EPS) * lng_ref[...] + lnb_ref[...]
    co_ref[...] = (yn * jax.nn.sigmoid(yn)).astype(co_ref.dtype)

    pos = i * ts + lax.broadcasted_iota(jnp.int32, (ts, 1), 0)
    for gi, size in enumerate(POOL_SIZES):
        half = size // 2
        c0 = gi * POOL_GROUP
        tot = jnp.dot(band_ref[gi], px_ref[:, c0:c0 + POOL_GROUP], preferred_element_type=F32)
        lo = jnp.clip(pos - half, 0, seq - 1)
        hi = jnp.clip(pos + half - 1, 0, seq - 1)
        cnt = (hi - lo + 1).astype(F32)
        mixed = (tot / cnt - pm_ref[:, c0:c0 + POOL_GROUP].astype(F32)).astype(BF16)
        yg = jnp.dot(mixed, pw_ref[gi], preferred_element_type=F32)
        po_ref[:, c0:c0 + POOL_GROUP] = (yg * ps_ref[:, c0:c0 + POOL_GROUP]).astype(po_ref.dtype)


def _pool_bands(ts):
    t = jnp.arange(ts)[:, None] + HALO
    r = jnp.arange(ts + 2 * HALO)[None, :]
    return jnp.stack([((r >= t - size // 2) & (r <= t + size // 2 - 1)).astype(BF16) for size in POOL_SIZES])


def _conv_pool(y, p, w_dw, b_dw, ln_g, ln_b, pool_w, pool_scale, l, batch, seq):
    y3 = y.reshape(batch, seq, CONV_WIDTH)
    p3 = p.reshape(batch, seq, POOL_WIDTH)
    per = CP_TS // HALO
    n_halo = seq // HALO
    ext = CP_TS + 2 * HALO

    def halo(width):
        prev = pl.BlockSpec((None, HALO, width), lambda b, i: (b, jnp.maximum(i * per - 1, 0), 0))
        main = pl.BlockSpec((None, CP_TS, width), lambda b, i: (b, i, 0))
        nxt = pl.BlockSpec((None, HALO, width), lambda b, i: (b, jnp.minimum((i + 1) * per, n_halo - 1), 0))
        return [prev, main, nxt]

    def layer(*shape):
        return pl.BlockSpec((None,) + shape, lambda b, i: (l,) + (0,) * len(shape))

    conv, pool = pl.pallas_call(
        _conv_pool_kernel,
        grid=(batch, seq // CP_TS),
        in_specs=halo(CONV_WIDTH) + halo(POOL_WIDTH) + [
            layer(CONV_SIZE, CONV_WIDTH), layer(1, CONV_WIDTH), layer(1, CONV_WIDTH), layer(1, CONV_WIDTH),
            pl.BlockSpec((N_POOL_GROUPS, CP_TS, ext), lambda b, i: (0, 0, 0)),
            layer(N_POOL_GROUPS, POOL_GROUP, POOL_GROUP), layer(1, POOL_WIDTH)],
        out_specs=[pl.BlockSpec((None, CP_TS, CONV_WIDTH), lambda b, i: (b, i, 0)),
                   pl.BlockSpec((None, CP_TS, POOL_WIDTH), lambda b, i: (b, i, 0))],
        out_shape=[jax.ShapeDtypeStruct((batch, seq, CONV_WIDTH), BF16),
                   jax.ShapeDtypeStruct((batch, seq, POOL_WIDTH), BF16)],
        scratch_shapes=[pltpu.VMEM((ext, CONV_WIDTH), F32),
                        pltpu.VMEM((ext, POOL_WIDTH), BF16),
                        pltpu.VMEM((SUBLANES - 1, ext - SUBLANES, CONV_WIDTH), F32),
                        pltpu.VMEM((CP_TS, CONV_WIDTH), F32)],
        compiler_params=_params("parallel", "arbitrary"),
        name="conv_pool",
    )(y3, y3, y3, p3, p3, p3, w_dw, b_dw[:, None, :], ln_g[:, None, :], ln_b[:, None, :],
      _pool_bands(CP_TS), pool_w, pool_scale[:, None, :])
    return conv.reshape(batch * seq, CONV_WIDTH), pool.reshape(batch * seq, POOL_WIDTH)


def _gated_merge_kernel(u_ref, at_ref, cv_ref, pl_ref, wg0_ref, wg1_ref, wg2_ref, wa_ref, wc_ref, wp_ref, o_ref):
    u = u_ref[...]

    def gate(w_ref):
        return jax.nn.sigmoid(jnp.dot(u, w_ref[...], preferred_element_type=F32))

    m = gate(wg0_ref) * jnp.dot(at_ref[...], wa_ref[...], preferred_element_type=F32)
    m = m + gate(wg1_ref) * jnp.dot(cv_ref[...], wc_ref[...], preferred_element_type=F32)
    m = m + gate(wg2_ref) * jnp.dot(pl_ref[...], wp_ref[...], preferred_element_type=F32)
    o_ref[...] = m.astype(o_ref.dtype)


def _gated_merge(u, attn, conv, pool, w_in, w_attn, w_conv, w_pool):
    t, d = u.shape
    row = lambda width: pl.BlockSpec((MERGE_TM, width), lambda i, n: (i, 0))
    return pl.pallas_call(
        _gated_merge_kernel,
        grid=(t // MERGE_TM, d // MERGE_TN),
        in_specs=[row(d), row(ATTN_WIDTH), row(CONV_WIDTH), row(POOL_WIDTH)]
        + [_cols(d, MERGE_TN, COL_GATES + b * d) for b in range(N_BRANCHES)]
        + [_cols(ATTN_WIDTH, MERGE_TN), _cols(CONV_WIDTH, MERGE_TN), _cols(POOL_WIDTH, MERGE_TN)],
        out_specs=pl.BlockSpec((MERGE_TM, MERGE_TN), lambda i, n: (i, n)),
        out_shape=jax.ShapeDtypeStruct((t, d), BF16),
        compiler_params=_params("parallel", "arbitrary"),
        name="gated_merge",
    )(u, attn, conv, pool, w_in, w_in, w_in, w_attn, w_conv, w_pool)


def _rope_lane_tables(seq):
    pos = jnp.arange(seq, dtype=F32)
    inv_freq = ROPE_THETA ** (-jnp.arange(0, ROT_DIM, 2, dtype=F32) / ROT_DIM)
    ang = pos[:, None] * inv_freq[None, :]
    cos, sin = jnp.cos(ang), jnp.sin(ang)
    ones = jnp.ones((seq, HEAD_DIM // 2 - ROT_HALF), F32)
    zeros = jnp.zeros((seq, HEAD_DIM // 2 - ROT_HALF), F32)
    cos_t = jnp.concatenate([cos, ones, cos, ones], axis=-1)
    sin_t = jnp.concatenate([-sin, zeros, sin, zeros], axis=-1)
    return cos_t, sin_t


def kernel(x, ffn1_pre_g, ffn1_post_g, ffn1_w_gate, ffn1_w_up, ffn1_w_down, mix_pre_g, mix_post_g, w_in, sink,
           w_attn_proj, conv_dw, conv_dw_b, conv_ln_g, conv_ln_b, w_conv_proj, pool_w, pool_scale, w_pool_proj,
           w_out, ffn2_pre_g, ffn2_post_g, ffn2_w_gate, ffn2_w_up, ffn2_w_down):
    batch, seq, d = x.shape
    cos_t, sin_t = _rope_lane_tables(seq)
    pool_w_b = pool_w.astype(BF16)

    h = x.reshape(batch * seq, d)
    xn, w_gate_b, w_up_b = _rmsnorm_bf16(h, ffn1_pre_g[0], casts=((ffn1_w_gate, 0), (ffn1_w_up, 0)))
    for l in range(DEPTH):
        act, w_down_b = _gate_up(xn, w_gate_b, w_up_b, ffn1_w_down, l)
        h, u, w_in_b, w_attn_b, w_conv_b, w_pool_b, w_out_b, w_qkv_b = _proj_residual(
            act, w_down_b, h, ffn1_post_g[l], mix_pre_g[l], 0.5, DOWN_TM,
            casts=((w_in, l), (w_attn_proj, l), (w_conv_proj, l), (w_pool_proj, l), (w_out, l)), emit_qkv=True)
        qkv = _qkv_proj(u, w_qkv_b, cos_t, sin_t, seq)
        y, p = _glu_pool_proj(u, w_in_b)
        attn = _windowed_attention(qkv, sink[l], batch, seq)
        conv, pool = _conv_pool(y, p, conv_dw, conv_dw_b, conv_ln_g, conv_ln_b, pool_w_b, pool_scale, l, batch, seq)
        merged = _gated_merge(u, attn, conv, pool, w_in_b, w_attn_b, w_conv_b, w_pool_b)
        h, xn, w_gate_b, w_up_b = _proj_residual(merged, w_out_b, h, mix_post_g[l], ffn2_pre_g[l], 1.0, OUT_TM,
                                                 casts=((ffn2_w_gate, l), (ffn2_w_up, l)))
        act, w_down_b = _gate_up(xn, w_gate_b, w_up_b, ffn2_w_down, l)
        if l + 1 < DEPTH:
            h, xn, w_gate_b, w_up_b = _proj_residual(act, w_down_b, h, ffn2_post_g[l], ffn1_pre_g[l + 1], 0.5, DOWN_TM,
                                                     casts=((ffn1_w_gate, l + 1), (ffn1_w_up, l + 1)))
        else:
            h, = _proj_residual(act, w_down_b, h, ffn2_post_g[l], None, 0.5, DOWN_TM)
    return h.reshape(batch, seq, d)
```

```python
import functools
import math

import jax
import jax.numpy as jnp
from jax import lax
from jax.experimental import pallas as pl
from jax.experimental.pallas import tpu as pltpu

D_MODEL = 2048
DEPTH = 4
HEAD_DIM = 128
N_Q_HEADS = D_MODEL // HEAD_DIM
N_KV_HEADS = N_Q_HEADS // 4
Q_PER_KV = N_Q_HEADS // N_KV_HEADS
ATTN_WIDTH = N_Q_HEADS * HEAD_DIM
KV_WIDTH = N_KV_HEADS * HEAD_DIM
WINDOW = 128
BLOCK = 128
ROPE_THETA = 500000.0
ROT_DIM = HEAD_DIM // 4
ROT_HALF = ROT_DIM // 2
CONV_WIDTH = D_MODEL // 2
CONV_SIZE = 31
CONV_PAD = CONV_SIZE // 2
POOL_WIDTH = D_MODEL // 2
POOL_SIZES = (2, 4, 8, 16)
N_POOL_GROUPS = len(POOL_SIZES)
POOL_GROUP = POOL_WIDTH // N_POOL_GROUPS
N_BRANCHES = 3
RMS_EPS = 1e-6
LN_EPS = 1e-5
NEG_INF = -1e30
LOG2_E = math.log2(math.e)
LOGIT_SCALE_LOG2 = (HEAD_DIM ** -0.5) * LOG2_E

QKV_WIDTH = ATTN_WIDTH + 2 * KV_WIDTH
COL_CONV_A = QKV_WIDTH
COL_CONV_G = COL_CONV_A + CONV_WIDTH
COL_POOL = COL_CONV_G + CONV_WIDTH
COL_GATES = COL_POOL + POOL_WIDTH

LANES = 128
SUBLANES = 8
MXU_COLS = 256
VMEM_LIMIT_BYTES = 56 * 1024 * 1024

NORM_ROWS = 512
UP_TM = 1024
UP_TF = 512
DOWN_TM = 256
OUT_TM = 512
RES_ROWS = 128
PROJ_TM = 1024
PROJ_TN = 512
QKV_TN = 1024
ATT_QB = 1024
HALO = 16
CP_TS = 256
CP_ROWS = 64
MERGE_TM = 1024
MERGE_TN = 256

F32 = jnp.float32
BF16 = jnp.bfloat16


def _params(*semantics):
    return pltpu.CompilerParams(dimension_semantics=semantics, vmem_limit_bytes=VMEM_LIMIT_BYTES)


def _rms_scale(x):
    return lax.rsqrt(jnp.sum(x * x, axis=-1, keepdims=True) * (1.0 / x.shape[-1]) + RMS_EPS)


def _cols(k, tn, col0=0):
    return pl.BlockSpec((k, tn), lambda i, j: (0, col0 // tn + j))


def _cast_specs(casts, n_steps, chunk):
    in_specs, args, out_specs, out_shape = [], [], [], []
    for src, src_l in casts:
        rows_total, cols = src.shape[1:]
        rows = rows_total // n_steps
        in_specs.append(pl.BlockSpec((None, rows, cols), lambda *g, src_l=src_l: (src_l, chunk(*g), 0)))
        args.append(src)
        out_specs.append(pl.BlockSpec((rows, cols), lambda *g: (chunk(*g), 0)))
        out_shape.append(jax.ShapeDtypeStruct((rows_total, cols), BF16))
    return in_specs, args, out_specs, out_shape


def _rmsnorm_kernel(h_ref, g_ref, *rest):
    n_cast = (len(rest) - 1) // 2
    o_ref = rest[n_cast]
    x = h_ref[...]
    o_ref[...] = (x * _rms_scale(x) * g_ref[...]).astype(o_ref.dtype)
    for src_ref, dst_ref in zip(rest[:n_cast], rest[n_cast + 1:]):
        dst_ref[...] = src_ref[...].astype(dst_ref.dtype)


def _rmsnorm_bf16(h, g, casts=()):
    t, d = h.shape
    n_steps = t // NORM_ROWS
    c_in, c_args, c_out, c_shape = _cast_specs(casts, n_steps, lambda i: i)
    return pl.pallas_call(
        _rmsnorm_kernel,
        grid=(n_steps,),
        in_specs=[pl.BlockSpec((NORM_ROWS, d), lambda i: (i, 0)),
                  pl.BlockSpec((1, d), lambda i: (0, 0))] + c_in,
        out_specs=[pl.BlockSpec((NORM_ROWS, d), lambda i: (i, 0))] + c_out,
        out_shape=[jax.ShapeDtypeStruct((t, d), BF16)] + c_shape,
        compiler_params=_params("parallel"),
        name="rmsnorm",
    )(h, g.reshape(1, d), *c_args)


def _gate_up_kernel(x_ref, wg_ref, wu_ref, wd_ref, o_ref, wd_bf_ref):
    x = x_ref[...]
    g = jnp.dot(x, wg_ref[...], preferred_element_type=F32)
    u = jnp.dot(x, wu_ref[...], preferred_element_type=F32)
    o_ref[...] = (g * jax.nn.sigmoid(g) * u).astype(o_ref.dtype)
    wd_bf_ref[...] = wd_ref[...].astype(wd_bf_ref.dtype)


def _gate_up(xn, w_gate, w_up, w_down, l):
    t, d = xn.shape
    ff = w_gate.shape[1]
    n_j = ff // UP_TF
    c_in, c_args, c_out, c_shape = _cast_specs(((w_down, l),), (t // UP_TM) * n_j, lambda i, j: i * n_j + j)
    return pl.pallas_call(
        _gate_up_kernel,
        grid=(t // UP_TM, n_j),
        in_specs=[pl.BlockSpec((UP_TM, d), lambda i, j: (i, 0)), _cols(d, UP_TF), _cols(d, UP_TF)] + c_in,
        out_specs=[pl.BlockSpec((UP_TM, UP_TF), lambda i, j: (i, j))] + c_out,
        out_shape=[jax.ShapeDtypeStruct((t, ff), BF16)] + c_shape,
        compiler_params=_params("parallel", "arbitrary"),
        name="gate_up",
    )(xn, w_gate, w_up, *c_args)


def _rotary_layout_block(x):
    lane = lax.broadcasted_iota(jnp.int32, (x.shape[0], HEAD_DIM), 1)
    from_above = (lane >= ROT_HALF) & (lane < HEAD_DIM // 2)
    from_below = (lane >= HEAD_DIM // 2) & (lane < HEAD_DIM // 2 + ROT_HALF)
    heads = []
    for c in range(0, x.shape[1], HEAD_DIM):
        xh = x[:, c:c + HEAD_DIM]
        up = pltpu.roll(xh, HEAD_DIM - ROT_HALF, axis=1)
        down = pltpu.roll(xh, HEAD_DIM // 2 - ROT_HALF, axis=1)
        heads.append(jnp.where(from_above, up, jnp.where(from_below, down, xh)))
    return jnp.concatenate(heads, axis=1)


def _proj_residual_kernel(a_ref, w_ref, h_ref, post_g_ref, *rest, res_scale, emit_next, n_cast, emit_qkv):
    rest = list(rest)
    next_g_ref = rest.pop(0) if emit_next else None
    cast_src_refs = [rest.pop(0) for _ in range(n_cast)]
    o_ref = rest.pop(0)
    nx_ref = rest.pop(0) if emit_next else None
    cast_dst_refs = [rest.pop(0) for _ in range(n_cast)]
    qkv_dst_ref = rest.pop(0) if emit_qkv else None
    z_even_ref, z_odd_ref = rest
    i = pl.program_id(0)
    tm = o_ref.shape[0]

    @pl.when(i == 0)
    def _():
        z_odd_ref[...] = jnp.zeros(z_odd_ref.shape, F32)

    for src_ref, dst_ref in zip(cast_src_refs, cast_dst_refs):
        dst_ref[...] = src_ref[...].astype(dst_ref.dtype)
    if emit_qkv:
        qk_width = ATTN_WIDTH + KV_WIDTH
        qkv_dst_ref[:, :qk_width] = _rotary_layout_block(cast_src_refs[0][:, :qk_width]).astype(qkv_dst_ref.dtype)
        qkv_dst_ref[:, qk_width:] = cast_src_refs[0][:, qk_width:QKV_WIDTH].astype(qkv_dst_ref.dtype)

    def step(z_cur_ref, z_prev_ref):
        z_cur_ref[...] = jnp.dot(a_ref[...], w_ref[...], preferred_element_type=F32)
        for r in range(0, tm, RES_ROWS):
            y = z_prev_ref[r:r + RES_ROWS, :]
            hn = h_ref[r:r + RES_ROWS, :] + res_scale * (y * _rms_scale(y) * post_g_ref[...])
            o_ref[r:r + RES_ROWS, :] = hn
            if emit_next:
                nx_ref[r:r + RES_ROWS, :] = (hn * _rms_scale(hn) * next_g_ref[...]).astype(nx_ref.dtype)

    @pl.when(i % 2 == 0)
    def _():
        step(z_even_ref, z_odd_ref)

    @pl.when(i % 2 == 1)
    def _():
        step(z_odd_ref, z_even_ref)


def _proj_residual(a, w, h, post_g, next_g, res_scale, tm, casts=(), emit_qkv=False):
    t, d = h.shape
    k = a.shape[1]
    n_tiles = t // tm
    emit_next = next_g is not None
    vec = pl.BlockSpec((1, d), lambda i: (0, 0))
    lagged = pl.BlockSpec((tm, d), lambda i: (jnp.maximum(i - 1, 0), 0))
    chunk = lambda i: jnp.minimum(i, n_tiles - 1)
    w_spec = pl.BlockSpec((k, d), lambda i: (0, 0), pipeline_mode=pl.Buffered(1))
    in_specs = [pl.BlockSpec((tm, k), lambda i: (chunk(i), 0)), w_spec, lagged, vec]
    args = [a, w, h, post_g.reshape(1, d)]
    out_specs, out_shape = [lagged], [jax.ShapeDtypeStruct((t, d), F32)]
    if emit_next:
        in_specs.append(vec)
        args.append(next_g.reshape(1, d))
        out_specs.append(lagged)
        out_shape.append(jax.ShapeDtypeStruct((t, d), BF16))
    c_in, c_args, c_out, c_shape = _cast_specs(casts, n_tiles, chunk)
    in_specs += c_in
    args += c_args
    out_specs += c_out
    out_shape += c_shape
    if emit_qkv:
        rows_total = casts[0][0].shape[1]
        out_specs.append(pl.BlockSpec((rows_total // n_tiles, QKV_WIDTH), lambda i: (chunk(i), 0)))
        out_shape.append(jax.ShapeDtypeStruct((rows_total, QKV_WIDTH), BF16))
    outs = pl.pallas_call(
        functools.partial(_proj_residual_kernel, res_scale=res_scale, emit_next=emit_next, n_cast=len(casts),
                          emit_qkv=emit_qkv),
        grid=(n_tiles + 1,),
        in_specs=in_specs, out_specs=out_specs, out_shape=out_shape,
        scratch_shapes=[pltpu.VMEM((tm, d), F32), pltpu.VMEM((tm, d), F32)],
        compiler_params=_params("arbitrary"),
        name="proj_residual",
    )(*args)
    return tuple(outs)


def _qkv_kernel(u_ref, w_ref, cos_ref, sin_ref, o_ref):
    j = pl.program_id(1)
    chunks_per_step = QKV_TN // MXU_COLS
    u = u_ref[...]
    for n in range(chunks_per_step):
        is_rot = j * chunks_per_step + n < (ATTN_WIDTH + KV_WIDTH) // MXU_COLS
        c = jnp.where(is_rot, cos_ref[...], 1.0)
        s = jnp.where(is_rot, sin_ref[...], 0.0)
        z = jnp.dot(u, w_ref[:, n * MXU_COLS:(n + 1) * MXU_COLS], preferred_element_type=F32)
        for a in range(0, MXU_COLS, HEAD_DIM):
            x = z[:, a:a + HEAD_DIM]
            partner = pltpu.roll(x, HEAD_DIM // 2, axis=1)
            col = n * MXU_COLS + a
            o_ref[:, col:col + HEAD_DIM] = (x * c + partner * s).astype(o_ref.dtype)


def _qkv_proj(u, w_qkv, cos_t, sin_t, seq):
    t, d = u.shape
    s_tiles = seq // PROJ_TM
    tab = pl.BlockSpec((PROJ_TM, HEAD_DIM), lambda i, j: (i % s_tiles, 0))
    return pl.pallas_call(
        _qkv_kernel,
        grid=(t // PROJ_TM, QKV_WIDTH // QKV_TN),
        in_specs=[pl.BlockSpec((PROJ_TM, d), lambda i, j: (i, 0)),
                  _cols(d, QKV_TN), tab, tab],
        out_specs=pl.BlockSpec((PROJ_TM, QKV_TN), lambda i, j: (i, j)),
        out_shape=jax.ShapeDtypeStruct((t, QKV_WIDTH), BF16),
        compiler_params=_params("parallel", "arbitrary"),
        name="qkv_proj",
    )(u, w_qkv, cos_t, sin_t)


def _glu_pool_kernel(u_ref, wa_ref, wg_ref, wp_ref, y_ref, p_ref):
    u = u_ref[...]
    a = jnp.dot(u, wa_ref[...], preferred_element_type=F32)
    g = jnp.dot(u, wg_ref[...], preferred_element_type=F32)
    y_ref[...] = (a * jax.nn.sigmoid(g)).astype(y_ref.dtype)
    p_ref[...] = jnp.dot(u, wp_ref[...], preferred_element_type=F32).astype(p_ref.dtype)


def _glu_pool_proj(u, w_in):
    t, d = u.shape
    out = pl.BlockSpec((PROJ_TM, PROJ_TN), lambda i, j: (i, j))
    return pl.pallas_call(
        _glu_pool_kernel,
        grid=(t // PROJ_TM, CONV_WIDTH // PROJ_TN),
        in_specs=[pl.BlockSpec((PROJ_TM, d), lambda i, j: (i, 0)),
                  _cols(d, PROJ_TN, COL_CONV_A),
                  _cols(d, PROJ_TN, COL_CONV_G),
                  _cols(d, PROJ_TN, COL_POOL)],
        out_specs=[out, out],
        out_shape=[jax.ShapeDtypeStruct((t, CONV_WIDTH), BF16),
                   jax.ShapeDtypeStruct((t, POOL_WIDTH), BF16)],
        compiler_params=_params("parallel", "arbitrary"),
        name="glu_pool_proj",
    )(u, w_in, w_in, w_in)


def _attn_kernel(sink_ref, q_ref, kp_ref, km_ref, kn_ref, vp_ref, vm_ref, vn_ref, o_ref,
                 kx_ref, vx_ref, wb_ref):
    i = pl.program_id(1)
    n_sub = ATT_QB // BLOCK
    seq = pl.num_programs(1) * ATT_QB
    keys = 3 * BLOCK

    kx_ref[0:BLOCK, :] = kp_ref[...]
    kx_ref[BLOCK:BLOCK + ATT_QB, :] = km_ref[...]
    kx_ref[BLOCK + ATT_QB:, :] = kn_ref[...]
    vx_ref[0:BLOCK, :] = vp_ref[...]
    vx_ref[BLOCK:BLOCK + ATT_QB, :] = vm_ref[...]
    vx_ref[BLOCK + ATT_QB:, :] = vn_ref[...]

    qi = lax.broadcasted_iota(jnp.int32, (BLOCK, keys), 0)
    kj = lax.broadcasted_iota(jnp.int32, (BLOCK, keys), 1)
    wb_ref[...] = jnp.where(jnp.abs(kj - BLOCK - qi) <= WINDOW, 0.0, NEG_INF).astype(F32)

    def sub_block(j, carry):
        r0 = pl.multiple_of(j * BLOCK, BLOCK)
        key_pos = (i * n_sub + j) * BLOCK - BLOCK + lax.broadcasted_iota(jnp.int32, (1, keys), 1)
        bias = wb_ref[...] + jnp.where((key_pos >= 0) & (key_pos < seq), 0.0, NEG_INF).astype(F32)
        for hh in range(N_KV_HEADS):
            kc = kx_ref[pl.ds(r0, keys), hh * HEAD_DIM:(hh + 1) * HEAD_DIM]
            vc = vx_ref[pl.ds(r0, keys), hh * HEAD_DIM:(hh + 1) * HEAD_DIM]
            qs = jnp.concatenate(
                [q_ref[pl.ds(r0, BLOCK), (hh * Q_PER_KV + g) * HEAD_DIM:(hh * Q_PER_KV + g + 1) * HEAD_DIM]
                 for g in range(Q_PER_KV)], axis=0)
            s = lax.dot_general(qs, kc, (((1,), (1,)), ((), ())), preferred_element_type=F32)
            probs, inv = [], []
            for g in range(Q_PER_KV):
                sk = sink_ref[hh * Q_PER_KV + g] * LOG2_E
                lg = s[g * BLOCK:(g + 1) * BLOCK, :] * LOGIT_SCALE_LOG2 + bias
                m = jnp.maximum(jnp.max(lg, axis=-1, keepdims=True), sk)
                p = jnp.exp2(lg - m)
                inv.append(1.0 / (jnp.sum(p, axis=-1, keepdims=True) + jnp.exp2(sk - m)))
                probs.append(p.astype(BF16))
            pv = jnp.dot(jnp.concatenate(probs, axis=0), vc, preferred_element_type=F32)
            for g in range(Q_PER_KV):
                c0 = (hh * Q_PER_KV + g) * HEAD_DIM
                o_ref[pl.ds(r0, BLOCK), c0:c0 + HEAD_DIM] = (
                    pv[g * BLOCK:(g + 1) * BLOCK, :] * inv[g]).astype(o_ref.dtype)
        return carry

    lax.fori_loop(0, n_sub, sub_block, 0, unroll=2)


def _windowed_attention(qkv, sink_l, batch, seq):
    qkv3 = qkv.reshape(batch, seq, QKV_WIDTH)
    n_sub = ATT_QB // BLOCK
    n_blk = seq // BLOCK
    k_col = ATTN_WIDTH // KV_WIDTH
    v_col = k_col + 1

    def halo(col):
        prev = pl.BlockSpec((None, BLOCK, KV_WIDTH), lambda b, i: (b, jnp.maximum(i * n_sub - 1, 0), col))
        main = pl.BlockSpec((None, ATT_QB, KV_WIDTH), lambda b, i: (b, i, col))
        nxt = pl.BlockSpec((None, BLOCK, KV_WIDTH), lambda b, i: (b, jnp.minimum((i + 1) * n_sub, n_blk - 1), col))
        return [prev, main, nxt]

    out = pl.pallas_call(
        _attn_kernel,
        grid=(batch, seq // ATT_QB),
        in_specs=[pl.BlockSpec(memory_space=pltpu.SMEM),
                  pl.BlockSpec((None, ATT_QB, ATTN_WIDTH), lambda b, i: (b, i, 0))] + halo(k_col) + halo(v_col),
        out_specs=pl.BlockSpec((None, ATT_QB, ATTN_WIDTH), lambda b, i: (b, i, 0)),
        out_shape=jax.ShapeDtypeStruct((batch, seq, ATTN_WIDTH), BF16),
        scratch_shapes=[pltpu.VMEM((ATT_QB + 2 * BLOCK, KV_WIDTH), BF16),
                        pltpu.VMEM((ATT_QB + 2 * BLOCK, KV_WIDTH), BF16),
                        pltpu.VMEM((BLOCK, 3 * BLOCK), F32)],
        compiler_params=_params("parallel", "arbitrary"),
        name="window_attn",
    )(sink_l, qkv3, qkv3, qkv3, qkv3, qkv3, qkv3, qkv3)
    return out.reshape(batch * seq, ATTN_WIDTH)


def _conv_pool_kernel(yp_ref, ym_ref, yn_ref, pp_ref, pm_ref, pn_ref, wdw_ref, bdw_ref, lng_ref, lnb_ref,
                      band_ref, pw_ref, ps_ref, co_ref, po_ref, yx_ref, px_ref, sh_ref, cb_ref):
    i = pl.program_id(1)
    n_i = pl.num_programs(1)
    ts = ym_ref.shape[0]
    seq = n_i * ts

    has_prev = i > 0
    has_next = i < n_i - 1
    yx_ref[0:HALO, :] = jnp.where(has_prev, yp_ref[...], 0).astype(F32)
    yx_ref[HALO:HALO + ts, :] = ym_ref[...].astype(F32)
    yx_ref[HALO + ts:, :] = jnp.where(has_next, yn_ref[...], 0).astype(F32)
    px_ref[0:HALO, :] = jnp.where(has_prev, pp_ref[...], 0)
    px_ref[HALO:HALO + ts, :] = pm_ref[...]
    px_ref[HALO + ts:, :] = jnp.where(has_next, pn_ref[...], 0)

    n_sh = sh_ref.shape[1]
    for b in range(1, SUBLANES):
        sh_ref[b - 1] = yx_ref[b:b + n_sh, :]

    for r in range(0, ts, CP_ROWS):
        for c in range(0, CONV_WIDTH, LANES):
            acc = jnp.zeros((CP_ROWS, LANES), F32)
            for k in range(CONV_SIZE):
                a, b = divmod(k + HALO - CONV_PAD, SUBLANES)
                src = yx_ref if b == 0 else sh_ref.at[b - 1]
                r_src = r + a * SUBLANES
                acc = acc + src[r_src:r_src + CP_ROWS, c:c + LANES] * wdw_ref[k:k + 1, c:c + LANES]
            cb_ref[r:r + CP_ROWS, c:c + LANES] = acc + bdw_ref[:, c:c + LANES]

    x = cb_ref[...]
    mu = jnp.sum(x, axis=-1, keepdims=True) * (1.0 / CONV_WIDTH)
    xc = x - mu
    var = jnp.sum(xc * xc, axis=-1, keepdims=True) * (1.0 / CONV_WIDTH)
    yn = xc * lax.rsqrt(var + LN_EPS) * lng_ref[...] + lnb_ref[...]
    co_ref[...] = (yn * jax.nn.sigmoid(yn)).astype(co_ref.dtype)

    pos = i * ts + lax.broadcasted_iota(jnp.int32, (ts, 1), 0)
    for gi, size in enumerate(POOL_SIZES):
        half = size // 2
        c0 = gi * POOL_GROUP
        tot = jnp.dot(band_ref[gi], px_ref[:, c0:c0 + POOL_GROUP], preferred_element_type=F32)
        lo = jnp.clip(pos - half, 0, seq - 1)
        hi = jnp.clip(pos + half - 1, 0, seq - 1)
        cnt = (hi - lo + 1).astype(F32)
        mixed = (tot / cnt - pm_ref[:, c0:c0 + POOL_GROUP].astype(F32)).astype(BF16)
        yg = jnp.dot(mixed, pw_ref[gi], preferred_element_type=F32)
        po_ref[:, c0:c0 + POOL_GROUP] = (yg * ps_ref[:, c0:c0 + POOL_GROUP]).astype(po_ref.dtype)


def _pool_bands(ts):
    t = jnp.arange(ts)[:, None] + HALO
    r = jnp.arange(ts + 2 * HALO)[None, :]
    return jnp.stack([((r >= t - size // 2) & (r <= t + size // 2 - 1)).astype(BF16) for size in POOL_SIZES])


def _conv_pool(y, p, w_dw, b_dw, ln_g, ln_b, pool_w, pool_scale, l, batch, seq):
    y3 = y.reshape(batch, seq, CONV_WIDTH)
    p3 = p.reshape(batch, seq, POOL_WIDTH)
    per = CP_TS // HALO
    n_halo = seq // HALO
    ext = CP_TS + 2 * HALO

    def halo(width):
        prev = pl.BlockSpec((None, HALO, width), lambda b, i: (b, jnp.maximum(i * per - 1, 0), 0))
        main = pl.BlockSpec((None, CP_TS, width), lambda b, i: (b, i, 0))
        nxt = pl.BlockSpec((None, HALO, width), lambda b, i: (b, jnp.minimum((i + 1) * per, n_halo - 1), 0))
        return [prev, main, nxt]

    def layer(*shape):
        return pl.BlockSpec((None,) + shape, lambda b, i: (l,) + (0,) * len(shape))

    conv, pool = pl.pallas_call(
        _conv_pool_kernel,
        grid=(batch, seq // CP_TS),
        in_specs=halo(CONV_WIDTH) + halo(POOL_WIDTH) + [
            layer(CONV_SIZE, CONV_WIDTH), layer(1, CONV_WIDTH), layer(1, CONV_WIDTH), layer(1, CONV_WIDTH),
            pl.BlockSpec((N_POOL_GROUPS, CP_TS, ext), lambda b, i: (0, 0, 0)),
            layer(N_POOL_GROUPS, POOL_GROUP, POOL_GROUP), layer(1, POOL_WIDTH)],
        out_specs=[pl.BlockSpec((None, CP_TS, CONV_WIDTH), lambda b, i: (b, i, 0)),
                   pl.BlockSpec((None, CP_TS, POOL_WIDTH), lambda b, i: (b, i, 0))],
        out_shape=[jax.ShapeDtypeStruct((batch, seq, CONV_WIDTH), BF16),
                   jax.ShapeDtypeStruct((batch, seq, POOL_WIDTH), BF16)],
        scratch_shapes=[pltpu.VMEM((ext, CONV_WIDTH), F32),
                        pltpu.VMEM((ext, POOL_WIDTH), BF16),
                        pltpu.VMEM((SUBLANES - 1, ext - SUBLANES, CONV_WIDTH), F32),
                        pltpu.VMEM((CP_TS, CONV_WIDTH), F32)],
        compiler_params=_params("parallel", "arbitrary"),
        name="conv_pool",
    )(y3, y3, y3, p3, p3, p3, w_dw, b_dw[:, None, :], ln_g[:, None, :], ln_b[:, None, :],
      _pool_bands(CP_TS), pool_w, pool_scale[:, None, :])
    return conv.reshape(batch * seq, CONV_WIDTH), pool.reshape(batch * seq, POOL_WIDTH)


def _gated_merge_kernel(u_ref, at_ref, cv_ref, pl_ref, wg0_ref, wg1_ref, wg2_ref, wa_ref, wc_ref, wp_ref, *rest):
    n_cast = (len(rest) - 1) // 2
    o_ref = rest[n_cast]
    u = u_ref[...]

    def gate(w_ref):
        return jax.nn.sigmoid(jnp.dot(u, w_ref[...], preferred_element_type=F32))

    m = gate(wg0_ref) * jnp.dot(at_ref[...], wa_ref[...], preferred_element_type=F32)
    m = m + gate(wg1_ref) * jnp.dot(cv_ref[...], wc_ref[...], preferred_element_type=F32)
    m = m + gate(wg2_ref) * jnp.dot(pl_ref[...], wp_ref[...], preferred_element_type=F32)
    o_ref[...] = m.astype(o_ref.dtype)
    for src_ref, dst_ref in zip(rest[:n_cast], rest[n_cast + 1:]):
        dst_ref[...] = src_ref[...].astype(dst_ref.dtype)


def _gated_merge(u, attn, conv, pool, w_in, w_attn, w_conv, w_pool, casts=()):
    t, d = u.shape
    n_n = d // MERGE_TN
    row = lambda width: pl.BlockSpec((MERGE_TM, width), lambda i, n: (i, 0))
    c_in, c_args, c_out, c_shape = _cast_specs(casts, (t // MERGE_TM) * n_n, lambda i, n: i * n_n + n)
    return pl.pallas_call(
        _gated_merge_kernel,
        grid=(t // MERGE_TM, n_n),
        in_specs=[row(d), row(ATTN_WIDTH), row(CONV_WIDTH), row(POOL_WIDTH)]
        + [_cols(d, MERGE_TN, COL_GATES + b * d) for b in range(N_BRANCHES)]
        + [_cols(ATTN_WIDTH, MERGE_TN), _cols(CONV_WIDTH, MERGE_TN), _cols(POOL_WIDTH, MERGE_TN)] + c_in,
        out_specs=[pl.BlockSpec((MERGE_TM, MERGE_TN), lambda i, n: (i, n))] + c_out,
        out_shape=[jax.ShapeDtypeStruct((t, d), BF16)] + c_shape,
        compiler_params=_params("parallel", "arbitrary"),
        name="gated_merge",
    )(u, attn, conv, pool, w_in, w_in, w_in, w_attn, w_conv, w_pool, *c_args)


def _rope_lane_tables(seq):
    pos = jnp.arange(seq, dtype=F32)
    inv_freq = ROPE_THETA ** (-jnp.arange(0, ROT_DIM, 2, dtype=F32) / ROT_DIM)
    ang = pos[:, None] * inv_freq[None, :]
    cos, sin = jnp.cos(ang), jnp.sin(ang)
    ones = jnp.ones((seq, HEAD_DIM // 2 - ROT_HALF), F32)
    zeros = jnp.zeros((seq, HEAD_DIM // 2 - ROT_HALF), F32)
    cos_t = jnp.concatenate([cos, ones, cos, ones], axis=-1)
    sin_t = jnp.concatenate([-sin, zeros, sin, zeros], axis=-1)
    return cos_t, sin_t


def kernel(x, ffn1_pre_g, ffn1_post_g, ffn1_w_gate, ffn1_w_up, ffn1_w_down, mix_pre_g, mix_post_g, w_in, sink,
           w_attn_proj, conv_dw, conv_dw_b, conv_ln_g, conv_ln_b, w_conv_proj, pool_w, pool_scale, w_pool_proj,
           w_out, ffn2_pre_g, ffn2_post_g, ffn2_w_gate, ffn2_w_up, ffn2_w_down):
    batch, seq, d = x.shape
    cos_t, sin_t = _rope_lane_tables(seq)
    pool_w_b = pool_w.astype(BF16)

    h = x.reshape(batch * seq, d)
    xn, w_gate_b, w_up_b = _rmsnorm_bf16(h, ffn1_pre_g[0], casts=((ffn1_w_gate, 0), (ffn1_w_up, 0)))
    for l in range(DEPTH):
        act, w_down_b = _gate_up(xn, w_gate_b, w_up_b, ffn1_w_down, l)
        h, u, w_in_b, w_attn_b, w_conv_b, w_pool_b, w_out_b, w_qkv_b = _proj_residual(
            act, w_down_b, h, ffn1_post_g[l], mix_pre_g[l], 0.5, DOWN_TM,
            casts=((w_in, l), (w_attn_proj, l), (w_conv_proj, l), (w_pool_proj, l), (w_out, l)), emit_qkv=True)
        qkv = _qkv_proj(u, w_qkv_b, cos_t, sin_t, seq)
        y, p = _glu_pool_proj(u, w_in_b)
        attn = _windowed_attention(qkv, sink[l], batch, seq)
        conv, pool = _conv_pool(y, p, conv_dw, conv_dw_b, conv_ln_g, conv_ln_b, pool_w_b, pool_scale, l, batch, seq)
        merged, w_gate_b, w_up_b = _gated_merge(u, attn, conv, pool, w_in_b, w_attn_b, w_conv_b, w_pool_b,
                                                casts=((ffn2_w_gate, l), (ffn2_w_up, l)))
        h, xn = _proj_residual(merged, w_out_b, h, mix_post_g[l], ffn2_pre_g[l], 1.0, OUT_TM)
        act, w_down_b = _gate_up(xn, w_gate_b, w_up_b, ffn2_w_down, l)
        if l + 1 < DEPTH:
            h, xn, w_gate_b, w_up_b = _proj_residual(act, w_down_b, h, ffn2_post_g[l], ffn1_pre_g[l + 1], 0.5, DOWN_TM,
                                                     casts=((ffn1_w_gate, l + 1), (ffn1_w_up, l + 1)))
        else:
            h, = _proj_residual(act, w_down_b, h, ffn2_post_g[l], None, 0.5, DOWN_TM)
    return h.reshape(batch, seq, d)
```

```python
import functools
import math

import jax
import jax.numpy as jnp
from jax import lax
from jax.experimental import pallas as pl
from jax.experimental.pallas import tpu as pltpu

D_MODEL = 2048
DEPTH = 4
HEAD_DIM = 128
N_Q_HEADS = D_MODEL // HEAD_DIM
N_KV_HEADS = N_Q_HEADS // 4
Q_PER_KV = N_Q_HEADS // N_KV_HEADS
ATTN_WIDTH = N_Q_HEADS * HEAD_DIM
KV_WIDTH = N_KV_HEADS * HEAD_DIM
WINDOW = 128
BLOCK = 128
ROPE_THETA = 500000.0
ROT_DIM = HEAD_DIM // 4
ROT_HALF = ROT_DIM // 2
CONV_WIDTH = D_MODEL // 2
CONV_SIZE = 31
CONV_PAD = CONV_SIZE // 2
POOL_WIDTH = D_MODEL // 2
POOL_SIZES = (2, 4, 8, 16)
N_POOL_GROUPS = len(POOL_SIZES)
POOL_GROUP = POOL_WIDTH // N_POOL_GROUPS
N_BRANCHES = 3
RMS_EPS = 1e-6
LN_EPS = 1e-5
NEG_INF = -1e30
LOG2_E = math.log2(math.e)
LOGIT_SCALE_LOG2 = (HEAD_DIM ** -0.5) * LOG2_E

QKV_WIDTH = ATTN_WIDTH + 2 * KV_WIDTH
COL_CONV_A = QKV_WIDTH
COL_CONV_G = COL_CONV_A + CONV_WIDTH
COL_POOL = COL_CONV_G + CONV_WIDTH
COL_GATES = COL_POOL + POOL_WIDTH

LANES = 128
SUBLANES = 8
MXU_COLS = 256
VMEM_LIMIT_BYTES = 56 * 1024 * 1024

NORM_ROWS = 512
UP_TM = 1024
UP_TF = 512
DOWN_TM = 256
OUT_TM = 512
RES_ROWS = 128
PROJ_TM = 1024
PROJ_TN = 512
QKV_TN = QKV_WIDTH
ATT_QB = 1024
HALO = 16
CP_TS = 256
CP_ROWS = 64
MERGE_TM = 1024
MERGE_TN = 256

F32 = jnp.float32
BF16 = jnp.bfloat16


def _params(*semantics):
    return pltpu.CompilerParams(dimension_semantics=semantics, vmem_limit_bytes=VMEM_LIMIT_BYTES)


def _rms_scale(x):
    return lax.rsqrt(jnp.sum(x * x, axis=-1, keepdims=True) * (1.0 / x.shape[-1]) + RMS_EPS)


def _cols(k, tn, col0=0):
    return pl.BlockSpec((k, tn), lambda i, j: (0, col0 // tn + j))


def _cast_specs(casts, n_steps, chunk):
    in_specs, args, out_specs, out_shape = [], [], [], []
    for src, src_l in casts:
        rows_total, cols = src.shape[1:]
        rows = rows_total // n_steps
        in_specs.append(pl.BlockSpec((None, rows, cols), lambda *g, src_l=src_l: (src_l, chunk(*g), 0)))
        args.append(src)
        out_specs.append(pl.BlockSpec((rows, cols), lambda *g: (chunk(*g), 0)))
        out_shape.append(jax.ShapeDtypeStruct((rows_total, cols), BF16))
    return in_specs, args, out_specs, out_shape


def _rmsnorm_kernel(h_ref, g_ref, *rest):
    n_cast = (len(rest) - 1) // 2
    o_ref = rest[n_cast]
    x = h_ref[...]
    o_ref[...] = (x * _rms_scale(x) * g_ref[...]).astype(o_ref.dtype)
    for src_ref, dst_ref in zip(rest[:n_cast], rest[n_cast + 1:]):
        dst_ref[...] = src_ref[...].astype(dst_ref.dtype)


def _rmsnorm_bf16(h, g, casts=()):
    t, d = h.shape
    n_steps = t // NORM_ROWS
    c_in, c_args, c_out, c_shape = _cast_specs(casts, n_steps, lambda i: i)
    return pl.pallas_call(
        _rmsnorm_kernel,
        grid=(n_steps,),
        in_specs=[pl.BlockSpec((NORM_ROWS, d), lambda i: (i, 0)),
                  pl.BlockSpec((1, d), lambda i: (0, 0))] + c_in,
        out_specs=[pl.BlockSpec((NORM_ROWS, d), lambda i: (i, 0))] + c_out,
        out_shape=[jax.ShapeDtypeStruct((t, d), BF16)] + c_shape,
        compiler_params=_params("parallel"),
        name="rmsnorm",
    )(h, g.reshape(1, d), *c_args)


def _gate_up_kernel(x_ref, wg_ref, wu_ref, wd_ref, o_ref, wd_bf_ref):
    x = x_ref[...]
    g = jnp.dot(x, wg_ref[...], preferred_element_type=F32)
    u = jnp.dot(x, wu_ref[...], preferred_element_type=F32)
    o_ref[...] = (g * jax.nn.sigmoid(g) * u).astype(o_ref.dtype)
    wd_bf_ref[...] = wd_ref[...].astype(wd_bf_ref.dtype)


def _gate_up(xn, w_gate, w_up, w_down, l):
    t, d = xn.shape
    ff = w_gate.shape[1]
    n_j = ff // UP_TF
    c_in, c_args, c_out, c_shape = _cast_specs(((w_down, l),), (t // UP_TM) * n_j, lambda i, j: i * n_j + j)
    return pl.pallas_call(
        _gate_up_kernel,
        grid=(t // UP_TM, n_j),
        in_specs=[pl.BlockSpec((UP_TM, d), lambda i, j: (i, 0)), _cols(d, UP_TF), _cols(d, UP_TF)] + c_in,
        out_specs=[pl.BlockSpec((UP_TM, UP_TF), lambda i, j: (i, j))] + c_out,
        out_shape=[jax.ShapeDtypeStruct((t, ff), BF16)] + c_shape,
        compiler_params=_params("parallel", "arbitrary"),
        name="gate_up",
    )(xn, w_gate, w_up, *c_args)


def _rotary_layout_block(x):
    lane = lax.broadcasted_iota(jnp.int32, (x.shape[0], HEAD_DIM), 1)
    from_above = (lane >= ROT_HALF) & (lane < HEAD_DIM // 2)
    from_below = (lane >= HEAD_DIM // 2) & (lane < HEAD_DIM // 2 + ROT_HALF)
    heads = []
    for c in range(0, x.shape[1], HEAD_DIM):
        xh = x[:, c:c + HEAD_DIM]
        up = pltpu.roll(xh, HEAD_DIM - ROT_HALF, axis=1)
        down = pltpu.roll(xh, HEAD_DIM // 2 - ROT_HALF, axis=1)
        heads.append(jnp.where(from_above, up, jnp.where(from_below, down, xh)))
    return jnp.concatenate(heads, axis=1)


def _proj_residual_kernel(a_ref, w_ref, h_ref, post_g_ref, *rest, res_scale, emit_next, n_cast, emit_qkv):
    rest = list(rest)
    next_g_ref = rest.pop(0) if emit_next else None
    cast_src_refs = [rest.pop(0) for _ in range(n_cast)]
    o_ref = rest.pop(0)
    nx_ref = rest.pop(0) if emit_next else None
    cast_dst_refs = [rest.pop(0) for _ in range(n_cast)]
    qkv_dst_ref = rest.pop(0) if emit_qkv else None
    z_even_ref, z_odd_ref = rest
    i = pl.program_id(0)
    tm = o_ref.shape[0]

    @pl.when(i == 0)
    def _():
        z_odd_ref[...] = jnp.zeros(z_odd_ref.shape, F32)

    for src_ref, dst_ref in zip(cast_src_refs, cast_dst_refs):
        dst_ref[...] = src_ref[...].astype(dst_ref.dtype)
    if emit_qkv:
        qk_width = ATTN_WIDTH + KV_WIDTH
        qkv_dst_ref[:, :qk_width] = _rotary_layout_block(cast_src_refs[0][:, :qk_width]).astype(qkv_dst_ref.dtype)
        qkv_dst_ref[:, qk_width:] = cast_src_refs[0][:, qk_width:QKV_WIDTH].astype(qkv_dst_ref.dtype)

    def step(z_cur_ref, z_prev_ref):
        z_cur_ref[...] = jnp.dot(a_ref[...], w_ref[...], preferred_element_type=F32)
        for r in range(0, tm, RES_ROWS):
            y = z_prev_ref[r:r + RES_ROWS, :]
            hn = h_ref[r:r + RES_ROWS, :] + res_scale * (y * _rms_scale(y) * post_g_ref[...])
            o_ref[r:r + RES_ROWS, :] = hn
            if emit_next:
                nx_ref[r:r + RES_ROWS, :] = (hn * _rms_scale(hn) * next_g_ref[...]).astype(nx_ref.dtype)

    @pl.when(i % 2 == 0)
    def _():
        step(z_even_ref, z_odd_ref)

    @pl.when(i % 2 == 1)
    def _():
        step(z_odd_ref, z_even_ref)


def _proj_residual(a, w, h, post_g, next_g, res_scale, tm, casts=(), emit_qkv=False):
    t, d = h.shape
    k = a.shape[1]
    n_tiles = t // tm
    emit_next = next_g is not None
    vec = pl.BlockSpec((1, d), lambda i: (0, 0))
    lagged = pl.BlockSpec((tm, d), lambda i: (jnp.maximum(i - 1, 0), 0))
    chunk = lambda i: jnp.minimum(i, n_tiles - 1)
    w_spec = pl.BlockSpec((k, d), lambda i: (0, 0), pipeline_mode=pl.Buffered(1))
    in_specs = [pl.BlockSpec((tm, k), lambda i: (chunk(i), 0)), w_spec, lagged, vec]
    args = [a, w, h, post_g.reshape(1, d)]
    out_specs, out_shape = [lagged], [jax.ShapeDtypeStruct((t, d), F32)]
    if emit_next:
        in_specs.append(vec)
        args.append(next_g.reshape(1, d))
        out_specs.append(lagged)
        out_shape.append(jax.ShapeDtypeStruct((t, d), BF16))
    c_in, c_args, c_out, c_shape = _cast_specs(casts, n_tiles, chunk)
    in_specs += c_in
    args += c_args
    out_specs += c_out
    out_shape += c_shape
    if emit_qkv:
        rows_total = casts[0][0].shape[1]
        out_specs.append(pl.BlockSpec((rows_total // n_tiles, QKV_WIDTH), lambda i: (chunk(i), 0)))
        out_shape.append(jax.ShapeDtypeStruct((rows_total, QKV_WIDTH), BF16))
    outs = pl.pallas_call(
        functools.partial(_proj_residual_kernel, res_scale=res_scale, emit_next=emit_next, n_cast=len(casts),
                          emit_qkv=emit_qkv),
        grid=(n_tiles + 1,),
        in_specs=in_specs, out_specs=out_specs, out_shape=out_shape,
        scratch_shapes=[pltpu.VMEM((tm, d), F32), pltpu.VMEM((tm, d), F32)],
        compiler_params=_params("arbitrary"),
        name="proj_residual",
    )(*args)
    return tuple(outs)


def _qkv_kernel(u_ref, w_ref, cos_ref, sin_ref, o_ref):
    j = pl.program_id(1)
    chunks_per_step = QKV_TN // MXU_COLS
    u = u_ref[...]
    for n in range(chunks_per_step):
        is_rot = j * chunks_per_step + n < (ATTN_WIDTH + KV_WIDTH) // MXU_COLS
        c = jnp.where(is_rot, cos_ref[...], 1.0)
        s = jnp.where(is_rot, sin_ref[...], 0.0)
        z = jnp.dot(u, w_ref[:, n * MXU_COLS:(n + 1) * MXU_COLS], preferred_element_type=F32)
        for a in range(0, MXU_COLS, HEAD_DIM):
            x = z[:, a:a + HEAD_DIM]
            partner = pltpu.roll(x, HEAD_DIM // 2, axis=1)
            col = n * MXU_COLS + a
            o_ref[:, col:col + HEAD_DIM] = (x * c + partner * s).astype(o_ref.dtype)


def _qkv_proj(u, w_qkv, cos_t, sin_t, seq):
    t, d = u.shape
    s_tiles = seq // PROJ_TM
    tab = pl.BlockSpec((PROJ_TM, HEAD_DIM), lambda i, j: (i % s_tiles, 0))
    return pl.pallas_call(
        _qkv_kernel,
        grid=(t // PROJ_TM, QKV_WIDTH // QKV_TN),
        in_specs=[pl.BlockSpec((PROJ_TM, d), lambda i, j: (i, 0)),
                  _cols(d, QKV_TN), tab, tab],
        out_specs=pl.BlockSpec((PROJ_TM, QKV_TN), lambda i, j: (i, j)),
        out_shape=jax.ShapeDtypeStruct((t, QKV_WIDTH), BF16),
        compiler_params=_params("parallel", "arbitrary"),
        name="qkv_proj",
    )(u, w_qkv, cos_t, sin_t)


def _glu_pool_kernel(u_ref, wa_ref, wg_ref, wp_ref, y_ref, p_ref):
    u = u_ref[...]
    a = jnp.dot(u, wa_ref[...], preferred_element_type=F32)
    g = jnp.dot(u, wg_ref[...], preferred_element_type=F32)
    y_ref[...] = (a * jax.nn.sigmoid(g)).astype(y_ref.dtype)
    p_ref[...] = jnp.dot(u, wp_ref[...], preferred_element_type=F32).astype(p_ref.dtype)


def _glu_pool_proj(u, w_in):
    t, d = u.shape
    out = pl.BlockSpec((PROJ_TM, PROJ_TN), lambda i, j: (i, j))
    return pl.pallas_call(
        _glu_pool_kernel,
        grid=(t // PROJ_TM, CONV_WIDTH // PROJ_TN),
        in_specs=[pl.BlockSpec((PROJ_TM, d), lambda i, j: (i, 0)),
                  _cols(d, PROJ_TN, COL_CONV_A),
                  _cols(d, PROJ_TN, COL_CONV_G),
                  _cols(d, PROJ_TN, COL_POOL)],
        out_specs=[out, out],
        out_shape=[jax.ShapeDtypeStruct((t, CONV_WIDTH), BF16),
                   jax.ShapeDtypeStruct((t, POOL_WIDTH), BF16)],
        compiler_params=_params("parallel", "arbitrary"),
        name="glu_pool_proj",
    )(u, w_in, w_in, w_in)


def _attn_kernel(sink_ref, q_ref, kp_ref, km_ref, kn_ref, vp_ref, vm_ref, vn_ref, o_ref,
                 kx_ref, vx_ref, wb_ref):
    i = pl.program_id(1)
    n_sub = ATT_QB // BLOCK
    seq = pl.num_programs(1) * ATT_QB
    keys = 3 * BLOCK

    kx_ref[0:BLOCK, :] = kp_ref[...]
    kx_ref[BLOCK:BLOCK + ATT_QB, :] = km_ref[...]
    kx_ref[BLOCK + ATT_QB:, :] = kn_ref[...]
    vx_ref[0:BLOCK, :] = vp_ref[...]
    vx_ref[BLOCK:BLOCK + ATT_QB, :] = vm_ref[...]
    vx_ref[BLOCK + ATT_QB:, :] = vn_ref[...]

    qi = lax.broadcasted_iota(jnp.int32, (BLOCK, keys), 0)
    kj = lax.broadcasted_iota(jnp.int32, (BLOCK, keys), 1)
    wb_ref[...] = jnp.where(jnp.abs(kj - BLOCK - qi) <= WINDOW, 0.0, NEG_INF).astype(F32)

    def sub_block(j, carry):
        r0 = pl.multiple_of(j * BLOCK, BLOCK)
        key_pos = (i * n_sub + j) * BLOCK - BLOCK + lax.broadcasted_iota(jnp.int32, (1, keys), 1)
        bias = wb_ref[...] + jnp.where((key_pos >= 0) & (key_pos < seq), 0.0, NEG_INF).astype(F32)
        for hh in range(N_KV_HEADS):
            kc = kx_ref[pl.ds(r0, keys), hh * HEAD_DIM:(hh + 1) * HEAD_DIM]
            vc = vx_ref[pl.ds(r0, keys), hh * HEAD_DIM:(hh + 1) * HEAD_DIM]
            qs = jnp.concatenate(
                [q_ref[pl.ds(r0, BLOCK), (hh * Q_PER_KV + g) * HEAD_DIM:(hh * Q_PER_KV + g + 1) * HEAD_DIM]
                 for g in range(Q_PER_KV)], axis=0)
            s = lax.dot_general(qs, kc, (((1,), (1,)), ((), ())), preferred_element_type=F32)
            probs, inv = [], []
            for g in range(Q_PER_KV):
                sk = sink_ref[hh * Q_PER_KV + g] * LOG2_E
                lg = s[g * BLOCK:(g + 1) * BLOCK, :] * LOGIT_SCALE_LOG2 + bias
                m = jnp.maximum(jnp.max(lg, axis=-1, keepdims=True), sk)
                p = jnp.exp2(lg - m)
                inv.append(1.0 / (jnp.sum(p, axis=-1, keepdims=True) + jnp.exp2(sk - m)))
                probs.append(p.astype(BF16))
            pv = jnp.dot(jnp.concatenate(probs, axis=0), vc, preferred_element_type=F32)
            for g in range(Q_PER_KV):
                c0 = (hh * Q_PER_KV + g) * HEAD_DIM
                o_ref[pl.ds(r0, BLOCK), c0:c0 + HEAD_DIM] = (
                    pv[g * BLOCK:(g + 1) * BLOCK, :] * inv[g]).astype(o_ref.dtype)
        return carry

    lax.fori_loop(0, n_sub, sub_block, 0, unroll=2)


def _windowed_attention(qkv, sink_l, batch, seq):
    qkv3 = qkv.reshape(batch, seq, QKV_WIDTH)
    n_sub = ATT_QB // BLOCK
    n_blk = seq // BLOCK
    k_col = ATTN_WIDTH // KV_WIDTH
    v_col = k_col + 1

    def halo(col):
        prev = pl.BlockSpec((None, BLOCK, KV_WIDTH), lambda b, i: (b, jnp.maximum(i * n_sub - 1, 0), col))
        main = pl.BlockSpec((None, ATT_QB, KV_WIDTH), lambda b, i: (b, i, col))
        nxt = pl.BlockSpec((None, BLOCK, KV_WIDTH), lambda b, i: (b, jnp.minimum((i + 1) * n_sub, n_blk - 1), col))
        return [prev, main, nxt]

    out = pl.pallas_call(
        _attn_kernel,
        grid=(batch, seq // ATT_QB),
        in_specs=[pl.BlockSpec(memory_space=pltpu.SMEM),
                  pl.BlockSpec((None, ATT_QB, ATTN_WIDTH), lambda b, i: (b, i, 0))] + halo(k_col) + halo(v_col),
        out_specs=pl.BlockSpec((None, ATT_QB, ATTN_WIDTH), lambda b, i: (b, i, 0)),
        out_shape=jax.ShapeDtypeStruct((batch, seq, ATTN_WIDTH), BF16),
        scratch_shapes=[pltpu.VMEM((ATT_QB + 2 * BLOCK, KV_WIDTH), BF16),
                        pltpu.VMEM((ATT_QB + 2 * BLOCK, KV_WIDTH), BF16),
                        pltpu.VMEM((BLOCK, 3 * BLOCK), F32)],
        compiler_params=_params("parallel", "arbitrary"),
        name="window_attn",
    )(sink_l, qkv3, qkv3, qkv3, qkv3, qkv3, qkv3, qkv3)
    return out.reshape(batch * seq, ATTN_WIDTH)


def _conv_pool_kernel(yp_ref, ym_ref, yn_ref, pp_ref, pm_ref, pn_ref, wdw_ref, bdw_ref, lng_ref, lnb_ref,
                      band_ref, pw_ref, ps_ref, co_ref, po_ref, yx_ref, px_ref, sh_ref, cb_ref):
    i = pl.program_id(1)
    n_i = pl.num_programs(1)
    ts = ym_ref.shape[0]
    seq = n_i * ts

    has_prev = i > 0
    has_next = i < n_i - 1
    yx_ref[0:HALO, :] = jnp.where(has_prev, yp_ref[...], 0).astype(F32)
    yx_ref[HALO:HALO + ts, :] = ym_ref[...].astype(F32)
    yx_ref[HALO + ts:, :] = jnp.where(has_next, yn_ref[...], 0).astype(F32)
    px_ref[0:HALO, :] = jnp.where(has_prev, pp_ref[...], 0)
    px_ref[HALO:HALO + ts, :] = pm_ref[...]
    px_ref[HALO + ts:, :] = jnp.where(has_next, pn_ref[...], 0)

    n_sh = sh_ref.shape[1]
    for b in range(1, SUBLANES):
        sh_ref[b - 1] = yx_ref[b:b + n_sh, :]

    for r in range(0, ts, CP_ROWS):
        for c in range(0, CONV_WIDTH, LANES):
            acc = jnp.zeros((CP_ROWS, LANES), F32)
            for k in range(CONV_SIZE):
                a, b = divmod(k + HALO - CONV_PAD, SUBLANES)
                src = yx_ref if b == 0 else sh_ref.at[b - 1]
                r_src = r + a * SUBLANES
                acc = acc + src[r_src:r_src + CP_ROWS, c:c + LANES] * wdw_ref[k:k + 1, c:c + LANES]
            cb_ref[r:r + CP_ROWS, c:c + LANES] = acc + bdw_ref[:, c:c + LANES]

    x = cb_ref[...]
    mu = jnp.sum(x, axis=-1, keepdims=True) * (1.0 / CONV_WIDTH)
    xc = x - mu
    var = jnp.sum(xc * xc, axis=-1, keepdims=True) * (1.0 / CONV_WIDTH)
    yn = xc * lax.rsqrt(var + LN_EPS) * lng_ref[...] + lnb_ref[...]
    co_ref[...] = (yn * jax.nn.sigmoid(yn)).astype(co_ref.dtype)

    pos = i * ts + lax.broadcasted_iota(jnp.int32, (ts, 1), 0)
    for gi, size in enumerate(POOL_SIZES):
        half = size // 2
        c0 = gi * POOL_GROUP
        tot = jnp.dot(band_ref[gi], px_ref[:, c0:c0 + POOL_GROUP], preferred_element_type=F32)
        lo = jnp.clip(pos - half, 0, seq - 1)
        hi = jnp.clip(pos + half - 1, 0, seq - 1)
        cnt = (hi - lo + 1).astype(F32)
        mixed = (tot / cnt - pm_ref[:, c0:c0 + POOL_GROUP].astype(F32)).astype(BF16)
        yg = jnp.dot(mixed, pw_ref[gi], preferred_element_type=F32)
        po_ref[:, c0:c0 + POOL_GROUP] = (yg * ps_ref[:, c0:c0 + POOL_GROUP]).astype(po_ref.dtype)


def _pool_bands(ts):
    t = jnp.arange(ts)[:, None] + HALO
    r = jnp.arange(ts + 2 * HALO)[None, :]
    return jnp.stack([((r >= t - size // 2) & (r <= t + size // 2 - 1)).astype(BF16) for size in POOL_SIZES])


def _conv_pool(y, p, w_dw, b_dw, ln_g, ln_b, pool_w, pool_scale, l, batch, seq):
    y3 = y.reshape(batch, seq, CONV_WIDTH)
    p3 = p.reshape(batch, seq, POOL_WIDTH)
    per = CP_TS // HALO
    n_halo = seq // HALO
    ext = CP_TS + 2 * HALO

    def halo(width):
        prev = pl.BlockSpec((None, HALO, width), lambda b, i: (b, jnp.maximum(i * per - 1, 0), 0))
        main = pl.BlockSpec((None, CP_TS, width), lambda b, i: (b, i, 0))
        nxt = pl.BlockSpec((None, HALO, width), lambda b, i: (b, jnp.minimum((i + 1) * per, n_halo - 1), 0))
        return [prev, main, nxt]

    def layer(*shape):
        return pl.BlockSpec((None,) + shape, lambda b, i: (l,) + (0,) * len(shape))

    conv, pool = pl.pallas_call(
        _conv_pool_kernel,
        grid=(batch, seq // CP_TS),
        in_specs=halo(CONV_WIDTH) + halo(POOL_WIDTH) + [
            layer(CONV_SIZE, CONV_WIDTH), layer(1, CONV_WIDTH), layer(1, CONV_WIDTH), layer(1, CONV_WIDTH),
            pl.BlockSpec((N_POOL_GROUPS, CP_TS, ext), lambda b, i: (0, 0, 0)),
            layer(N_POOL_GROUPS, POOL_GROUP, POOL_GROUP), layer(1, POOL_WIDTH)],
        out_specs=[pl.BlockSpec((None, CP_TS, CONV_WIDTH), lambda b, i: (b, i, 0)),
                   pl.BlockSpec((None, CP_TS, POOL_WIDTH), lambda b, i: (b, i, 0))],
        out_shape=[jax.ShapeDtypeStruct((batch, seq, CONV_WIDTH), BF16),
                   jax.ShapeDtypeStruct((batch, seq, POOL_WIDTH), BF16)],
        scratch_shapes=[pltpu.VMEM((ext, CONV_WIDTH), F32),
                        pltpu.VMEM((ext, POOL_WIDTH), BF16),
                        pltpu.VMEM((SUBLANES - 1, ext - SUBLANES, CONV_WIDTH), F32),
                        pltpu.VMEM((CP_TS, CONV_WIDTH), F32)],
        compiler_params=_params("parallel", "arbitrary"),
        name="conv_pool",
    )(y3, y3, y3, p3, p3, p3, w_dw, b_dw[:, None, :], ln_g[:, None, :], ln_b[:, None, :],
      _pool_bands(CP_TS), pool_w, pool_scale[:, None, :])
    return conv.reshape(batch * seq, CONV_WIDTH), pool.reshape(batch * seq, POOL_WIDTH)


def _gated_merge_kernel(u_ref, at_ref, cv_ref, pl_ref, wg0_ref, wg1_ref, wg2_ref, wa_ref, wc_ref, wp_ref, *rest):
    n_cast = (len(rest) - 1) // 2
    o_ref = rest[n_cast]
    u = u_ref[...]

    def gate(w_ref):
        return jax.nn.sigmoid(jnp.dot(u, w_ref[...], preferred_element_type=F32))

    m = gate(wg0_ref) * jnp.dot(at_ref[...], wa_ref[...], preferred_element_type=F32)
    m = m + gate(wg1_ref) * jnp.dot(cv_ref[...], wc_ref[...], preferred_element_type=F32)
    m = m + gate(wg2_ref) * jnp.dot(pl_ref[...], wp_ref[...], preferred_element_type=F32)
    o_ref[...] = m.astype(o_ref.dtype)
    for src_ref, dst_ref in zip(rest[:n_cast], rest[n_cast + 1:]):
        dst_ref[...] = src_ref[...].astype(dst_ref.dtype)


def _gated_merge(u, attn, conv, pool, w_in, w_attn, w_conv, w_pool, casts=()):
    t, d = u.shape
    n_n = d // MERGE_TN
    row = lambda width: pl.BlockSpec((MERGE_TM, width), lambda i, n: (i, 0))
    c_in, c_args, c_out, c_shape = _cast_specs(casts, (t // MERGE_TM) * n_n, lambda i, n: i * n_n + n)
    return pl.pallas_call(
        _gated_merge_kernel,
        grid=(t // MERGE_TM, n_n),
        in_specs=[row(d), row(ATTN_WIDTH), row(CONV_WIDTH), row(POOL_WIDTH)]
        + [_cols(d, MERGE_TN, COL_GATES + b * d) for b in range(N_BRANCHES)]
        + [_cols(ATTN_WIDTH, MERGE_TN), _cols(CONV_WIDTH, MERGE_TN), _cols(POOL_WIDTH, MERGE_TN)] + c_in,
        out_specs=[pl.BlockSpec((MERGE_TM, MERGE_TN), lambda i, n: (i, n))] + c_out,
        out_shape=[jax.ShapeDtypeStruct((t, d), BF16)] + c_shape,
        compiler_params=_params("parallel", "arbitrary"),
        name="gated_merge",
    )(u, attn, conv, pool, w_in, w_in, w_in, w_attn, w_conv, w_pool, *c_args)


def _rope_lane_tables(seq):
    pos = jnp.arange(seq, dtype=F32)
    inv_freq = ROPE_THETA ** (-jnp.arange(0, ROT_DIM, 2, dtype=F32) / ROT_DIM)
    ang = pos[:, None] * inv_freq[None, :]
    cos, sin = jnp.cos(ang), jnp.sin(ang)
    ones = jnp.ones((seq, HEAD_DIM // 2 - ROT_HALF), F32)
    zeros = jnp.zeros((seq, HEAD_DIM // 2 - ROT_HALF), F32)
    cos_t = jnp.concatenate([cos, ones, cos, ones], axis=-1)
    sin_t = jnp.concatenate([-sin, zeros, sin, zeros], axis=-1)
    return cos_t, sin_t


def kernel(x, ffn1_pre_g, ffn1_post_g, ffn1_w_gate, ffn1_w_up, ffn1_w_down, mix_pre_g, mix_post_g, w_in, sink,
           w_attn_proj, conv_dw, conv_dw_b, conv_ln_g, conv_ln_b, w_conv_proj, pool_w, pool_scale, w_pool_proj,
           w_out, ffn2_pre_g, ffn2_post_g, ffn2_w_gate, ffn2_w_up, ffn2_w_down):
    batch, seq, d = x.shape
    cos_t, sin_t = _rope_lane_tables(seq)
    pool_w_b = pool_w.astype(BF16)

    h = x.reshape(batch * seq, d)
    xn, w_gate_b, w_up_b = _rmsnorm_bf16(h, ffn1_pre_g[0], casts=((ffn1_w_gate, 0), (ffn1_w_up, 0)))
    for l in range(DEPTH):
        act, w_down_b = _gate_up(xn, w_gate_b, w_up_b, ffn1_w_down, l)
        h, u, w_in_b, w_attn_b, w_conv_b, w_pool_b, w_out_b, w_qkv_b = _proj_residual(
            act, w_down_b, h, ffn1_post_g[l], mix_pre_g[l], 0.5, DOWN_TM,
            casts=((w_in, l), (w_attn_proj, l), (w_conv_proj, l), (w_pool_proj, l), (w_out, l)), emit_qkv=True)
        qkv = _qkv_proj(u, w_qkv_b, cos_t, sin_t, seq)
        y, p = _glu_pool_proj(u, w_in_b)
        attn = _windowed_attention(qkv, sink[l], batch, seq)
        conv, pool = _conv_pool(y, p, conv_dw, conv_dw_b, conv_ln_g, conv_ln_b, pool_w_b, pool_scale, l, batch, seq)
        merged, w_gate_b, w_up_b = _gated_merge(u, attn, conv, pool, w_in_b, w_attn_b, w_conv_b, w_pool_b,
                                                casts=((ffn2_w_gate, l), (ffn2_w_up, l)))
        h, xn = _proj_residual(merged, w_out_b, h, mix_post_g[l], ffn2_pre_g[l], 1.0, OUT_TM)
        act, w_down_b = _gate_up(xn, w_gate_b, w_up_b, ffn2_w_down, l)
        if l + 1 < DEPTH:
            h, xn, w_gate_b, w_up_b = _proj_residual(act, w_down_b, h, ffn2_post_g[l], ffn1_pre_g[l + 1], 0.5, DOWN_TM,
                                                     casts=((ffn1_w_gate, l + 1), (ffn1_w_up, l + 1)))
        else:
            h, = _proj_residual(act, w_down_b, h, ffn2_post_g[l], None, 0.5, DOWN_TM)
    return h.reshape(batch, seq, d)
```

```python
import functools
import math

import jax
import jax.numpy as jnp
from jax import lax
from jax.experimental import pallas as pl
from jax.experimental.pallas import tpu as pltpu

D_MODEL = 2048
DEPTH = 4
HEAD_DIM = 128
N_Q_HEADS = D_MODEL // HEAD_DIM
N_KV_HEADS = N_Q_HEADS // 4
Q_PER_KV = N_Q_HEADS // N_KV_HEADS
ATTN_WIDTH = N_Q_HEADS * HEAD_DIM
KV_WIDTH = N_KV_HEADS * HEAD_DIM
WINDOW = 128
BLOCK = 128
ROPE_THETA = 500000.0
ROT_DIM = HEAD_DIM // 4
ROT_HALF = ROT_DIM // 2
CONV_WIDTH = D_MODEL // 2
CONV_SIZE = 31
CONV_PAD = CONV_SIZE // 2
POOL_WIDTH = D_MODEL // 2
POOL_SIZES = (2, 4, 8, 16)
N_POOL_GROUPS = len(POOL_SIZES)
POOL_GROUP = POOL_WIDTH // N_POOL_GROUPS
N_BRANCHES = 3
RMS_EPS = 1e-6
LN_EPS = 1e-5
NEG_INF = -1e30
LOG2_E = math.log2(math.e)
LOGIT_SCALE_LOG2 = (HEAD_DIM ** -0.5) * LOG2_E

QKV_WIDTH = ATTN_WIDTH + 2 * KV_WIDTH
COL_CONV_A = QKV_WIDTH
COL_CONV_G = COL_CONV_A + CONV_WIDTH
COL_POOL = COL_CONV_G + CONV_WIDTH
COL_GATES = COL_POOL + POOL_WIDTH

LANES = 128
SUBLANES = 8
MXU_COLS = 256
VMEM_LIMIT_BYTES = 56 * 1024 * 1024

NORM_ROWS = 512
UP_TM = 1024
UP_TF = 512
DOWN_TM = 256
OUT_TM = 512
RES_ROWS = 128
PROJ_TM = 1024
PROJ_TN = 1024
QKV_TN = QKV_WIDTH
ATT_QB = 1024
HALO = 16
CP_TS = 256
CP_ROWS = 64
MERGE_TM = 1024
MERGE_TN = 256

F32 = jnp.float32
BF16 = jnp.bfloat16


def _params(*semantics):
    return pltpu.CompilerParams(dimension_semantics=semantics, vmem_limit_bytes=VMEM_LIMIT_BYTES)


def _rms_scale(x):
    return lax.rsqrt(jnp.sum(x * x, axis=-1, keepdims=True) * (1.0 / x.shape[-1]) + RMS_EPS)


def _cols(k, tn, col0=0):
    return pl.BlockSpec((k, tn), lambda i, j: (0, col0 // tn + j))


def _cast_specs(casts, n_steps, chunk):
    in_specs, args, out_specs, out_shape = [], [], [], []
    for src, src_l in casts:
        rows_total, cols = src.shape[1:]
        rows = rows_total // n_steps
        in_specs.append(pl.BlockSpec((None, rows, cols), lambda *g, src_l=src_l: (src_l, chunk(*g), 0)))
        args.append(src)
        out_specs.append(pl.BlockSpec((rows, cols), lambda *g: (chunk(*g), 0)))
        out_shape.append(jax.ShapeDtypeStruct((rows_total, cols), BF16))
    return in_specs, args, out_specs, out_shape


def _rmsnorm_kernel(h_ref, g_ref, *rest):
    n_cast = (len(rest) - 1) // 2
    o_ref = rest[n_cast]
    x = h_ref[...]
    o_ref[...] = (x * _rms_scale(x) * g_ref[...]).astype(o_ref.dtype)
    for src_ref, dst_ref in zip(rest[:n_cast], rest[n_cast + 1:]):
        dst_ref[...] = src_ref[...].astype(dst_ref.dtype)


def _rmsnorm_bf16(h, g, casts=()):
    t, d = h.shape
    n_steps = t // NORM_ROWS
    c_in, c_args, c_out, c_shape = _cast_specs(casts, n_steps, lambda i: i)
    return pl.pallas_call(
        _rmsnorm_kernel,
        grid=(n_steps,),
        in_specs=[pl.BlockSpec((NORM_ROWS, d), lambda i: (i, 0)),
                  pl.BlockSpec((1, d), lambda i: (0, 0))] + c_in,
        out_specs=[pl.BlockSpec((NORM_ROWS, d), lambda i: (i, 0))] + c_out,
        out_shape=[jax.ShapeDtypeStruct((t, d), BF16)] + c_shape,
        compiler_params=_params("parallel"),
        name="rmsnorm",
    )(h, g.reshape(1, d), *c_args)


def _gate_up_kernel(x_ref, wg_ref, wu_ref, wd_ref, o_ref, wd_bf_ref):
    x = x_ref[...]
    g = jnp.dot(x, wg_ref[...], preferred_element_type=F32)
    u = jnp.dot(x, wu_ref[...], preferred_element_type=F32)
    o_ref[...] = (g * jax.nn.sigmoid(g) * u).astype(o_ref.dtype)
    wd_bf_ref[...] = wd_ref[...].astype(wd_bf_ref.dtype)


def _gate_up(xn, w_gate, w_up, w_down, l):
    t, d = xn.shape
    ff = w_gate.shape[1]
    n_j = ff // UP_TF
    c_in, c_args, c_out, c_shape = _cast_specs(((w_down, l),), (t // UP_TM) * n_j, lambda i, j: i * n_j + j)
    return pl.pallas_call(
        _gate_up_kernel,
        grid=(t // UP_TM, n_j),
        in_specs=[pl.BlockSpec((UP_TM, d), lambda i, j: (i, 0)), _cols(d, UP_TF), _cols(d, UP_TF)] + c_in,
        out_specs=[pl.BlockSpec((UP_TM, UP_TF), lambda i, j: (i, j))] + c_out,
        out_shape=[jax.ShapeDtypeStruct((t, ff), BF16)] + c_shape,
        compiler_params=_params("parallel", "arbitrary"),
        name="gate_up",
    )(xn, w_gate, w_up, *c_args)


def _rotary_layout_block(x):
    lane = lax.broadcasted_iota(jnp.int32, (x.shape[0], HEAD_DIM), 1)
    from_above = (lane >= ROT_HALF) & (lane < HEAD_DIM // 2)
    from_below = (lane >= HEAD_DIM // 2) & (lane < HEAD_DIM // 2 + ROT_HALF)
    heads = []
    for c in range(0, x.shape[1], HEAD_DIM):
        xh = x[:, c:c + HEAD_DIM]
        up = pltpu.roll(xh, HEAD_DIM - ROT_HALF, axis=1)
        down = pltpu.roll(xh, HEAD_DIM // 2 - ROT_HALF, axis=1)
        heads.append(jnp.where(from_above, up, jnp.where(from_below, down, xh)))
    return jnp.concatenate(heads, axis=1)


def _proj_residual_kernel(a_ref, w_ref, h_ref, post_g_ref, *rest, res_scale, emit_next, n_cast, emit_qkv):
    rest = list(rest)
    next_g_ref = rest.pop(0) if emit_next else None
    cast_src_refs = [rest.pop(0) for _ in range(n_cast)]
    o_ref = rest.pop(0)
    nx_ref = rest.pop(0) if emit_next else None
    cast_dst_refs = [rest.pop(0) for _ in range(n_cast)]
    qkv_dst_ref = rest.pop(0) if emit_qkv else None
    z_even_ref, z_odd_ref = rest
    i = pl.program_id(0)
    tm = o_ref.shape[0]

    @pl.when(i == 0)
    def _():
        z_odd_ref[...] = jnp.zeros(z_odd_ref.shape, F32)

    for src_ref, dst_ref in zip(cast_src_refs, cast_dst_refs):
        dst_ref[...] = src_ref[...].astype(dst_ref.dtype)
    if emit_qkv:
        qk_width = ATTN_WIDTH + KV_WIDTH
        qkv_dst_ref[:, :qk_width] = _rotary_layout_block(cast_src_refs[0][:, :qk_width]).astype(qkv_dst_ref.dtype)
        qkv_dst_ref[:, qk_width:] = cast_src_refs[0][:, qk_width:QKV_WIDTH].astype(qkv_dst_ref.dtype)

    def step(z_cur_ref, z_prev_ref):
        z_cur_ref[...] = jnp.dot(a_ref[...], w_ref[...], preferred_element_type=F32)
        for r in range(0, tm, RES_ROWS):
            y = z_prev_ref[r:r + RES_ROWS, :]
            hn = h_ref[r:r + RES_ROWS, :] + res_scale * (y * _rms_scale(y) * post_g_ref[...])
            o_ref[r:r + RES_ROWS, :] = hn
            if emit_next:
                nx_ref[r:r + RES_ROWS, :] = (hn * _rms_scale(hn) * next_g_ref[...]).astype(nx_ref.dtype)

    @pl.when(i % 2 == 0)
    def _():
        step(z_even_ref, z_odd_ref)

    @pl.when(i % 2 == 1)
    def _():
        step(z_odd_ref, z_even_ref)


def _proj_residual(a, w, h, post_g, next_g, res_scale, tm, casts=(), emit_qkv=False):
    t, d = h.shape
    k = a.shape[1]
    n_tiles = t // tm
    emit_next = next_g is not None
    vec = pl.BlockSpec((1, d), lambda i: (0, 0))
    lagged = pl.BlockSpec((tm, d), lambda i: (jnp.maximum(i - 1, 0), 0))
    chunk = lambda i: jnp.minimum(i, n_tiles - 1)
    w_spec = pl.BlockSpec((k, d), lambda i: (0, 0), pipeline_mode=pl.Buffered(1))
    in_specs = [pl.BlockSpec((tm, k), lambda i: (chunk(i), 0)), w_spec, lagged, vec]
    args = [a, w, h, post_g.reshape(1, d)]
    out_specs, out_shape = [lagged], [jax.ShapeDtypeStruct((t, d), F32)]
    if emit_next:
        in_specs.append(vec)
        args.append(next_g.reshape(1, d))
        out_specs.append(lagged)
        out_shape.append(jax.ShapeDtypeStruct((t, d), BF16))
    c_in, c_args, c_out, c_shape = _cast_specs(casts, n_tiles, chunk)
    in_specs += c_in
    args += c_args
    out_specs += c_out
    out_shape += c_shape
    if emit_qkv:
        rows_total = casts[0][0].shape[1]
        out_specs.append(pl.BlockSpec((rows_total // n_tiles, QKV_WIDTH), lambda i: (chunk(i), 0)))
        out_shape.append(jax.ShapeDtypeStruct((rows_total, QKV_WIDTH), BF16))
    outs = pl.pallas_call(
        functools.partial(_proj_residual_kernel, res_scale=res_scale, emit_next=emit_next, n_cast=len(casts),
                          emit_qkv=emit_qkv),
        grid=(n_tiles + 1,),
        in_specs=in_specs, out_specs=out_specs, out_shape=out_shape,
        scratch_shapes=[pltpu.VMEM((tm, d), F32), pltpu.VMEM((tm, d), F32)],
        compiler_params=_params("arbitrary"),
        name="proj_residual",
    )(*args)
    return tuple(outs)


def _qkv_kernel(u_ref, w_ref, cos_ref, sin_ref, o_ref):
    j = pl.program_id(1)
    chunks_per_step = QKV_TN // MXU_COLS
    u = u_ref[...]
    for n in range(chunks_per_step):
        is_rot = j * chunks_per_step + n < (ATTN_WIDTH + KV_WIDTH) // MXU_COLS
        c = jnp.where(is_rot, cos_ref[...], 1.0)
        s = jnp.where(is_rot, sin_ref[...], 0.0)
        z = jnp.dot(u, w_ref[:, n * MXU_COLS:(n + 1) * MXU_COLS], preferred_element_type=F32)
        for a in range(0, MXU_COLS, HEAD_DIM):
            x = z[:, a:a + HEAD_DIM]
            partner = pltpu.roll(x, HEAD_DIM // 2, axis=1)
            col = n * MXU_COLS + a
            o_ref[:, col:col + HEAD_DIM] = (x * c + partner * s).astype(o_ref.dtype)


def _qkv_proj(u, w_qkv, cos_t, sin_t, seq):
    t, d = u.shape
    s_tiles = seq // PROJ_TM
    tab = pl.BlockSpec((PROJ_TM, HEAD_DIM), lambda i, j: (i % s_tiles, 0))
    return pl.pallas_call(
        _qkv_kernel,
        grid=(t // PROJ_TM, QKV_WIDTH // QKV_TN),
        in_specs=[pl.BlockSpec((PROJ_TM, d), lambda i, j: (i, 0)),
                  _cols(d, QKV_TN), tab, tab],
        out_specs=pl.BlockSpec((PROJ_TM, QKV_TN), lambda i, j: (i, j)),
        out_shape=jax.ShapeDtypeStruct((t, QKV_WIDTH), BF16),
        compiler_params=_params("parallel", "arbitrary"),
        name="qkv_proj",
    )(u, w_qkv, cos_t, sin_t)


def _glu_pool_kernel(u_ref, wa_ref, wg_ref, wp_ref, y_ref, p_ref):
    u = u_ref[...]
    a = jnp.dot(u, wa_ref[...], preferred_element_type=F32)
    g = jnp.dot(u, wg_ref[...], preferred_element_type=F32)
    y_ref[...] = (a * jax.nn.sigmoid(g)).astype(y_ref.dtype)
    p_ref[...] = jnp.dot(u, wp_ref[...], preferred_element_type=F32).astype(p_ref.dtype)


def _glu_pool_proj(u, w_in):
    t, d = u.shape
    out = pl.BlockSpec((PROJ_TM, PROJ_TN), lambda i, j: (i, j))
    return pl.pallas_call(
        _glu_pool_kernel,
        grid=(t // PROJ_TM, CONV_WIDTH // PROJ_TN),
        in_specs=[pl.BlockSpec((PROJ_TM, d), lambda i, j: (i, 0)),
                  _cols(d, PROJ_TN, COL_CONV_A),
                  _cols(d, PROJ_TN, COL_CONV_G),
                  _cols(d, PROJ_TN, COL_POOL)],
        out_specs=[out, out],
        out_shape=[jax.ShapeDtypeStruct((t, CONV_WIDTH), BF16),
                   jax.ShapeDtypeStruct((t, POOL_WIDTH), BF16)],
        compiler_params=_params("parallel", "arbitrary"),
        name="glu_pool_proj",
    )(u, w_in, w_in, w_in)


def _attn_kernel(sink_ref, q_ref, kp_ref, km_ref, kn_ref, vp_ref, vm_ref, vn_ref, o_ref,
                 kx_ref, vx_ref, wb_ref):
    i = pl.program_id(1)
    n_sub = ATT_QB // BLOCK
    seq = pl.num_programs(1) * ATT_QB
    keys = 3 * BLOCK

    kx_ref[0:BLOCK, :] = kp_ref[...]
    kx_ref[BLOCK:BLOCK + ATT_QB, :] = km_ref[...]
    kx_ref[BLOCK + ATT_QB:, :] = kn_ref[...]
    vx_ref[0:BLOCK, :] = vp_ref[...]
    vx_ref[BLOCK:BLOCK + ATT_QB, :] = vm_ref[...]
    vx_ref[BLOCK + ATT_QB:, :] = vn_ref[...]

    qi = lax.broadcasted_iota(jnp.int32, (BLOCK, keys), 0)
    kj = lax.broadcasted_iota(jnp.int32, (BLOCK, keys), 1)
    wb_ref[...] = jnp.where(jnp.abs(kj - BLOCK - qi) <= WINDOW, 0.0, NEG_INF).astype(F32)

    def sub_block(j, carry):
        r0 = pl.multiple_of(j * BLOCK, BLOCK)
        key_pos = (i * n_sub + j) * BLOCK - BLOCK + lax.broadcasted_iota(jnp.int32, (1, keys), 1)
        bias = wb_ref[...] + jnp.where((key_pos >= 0) & (key_pos < seq), 0.0, NEG_INF).astype(F32)
        for hh in range(N_KV_HEADS):
            kc = kx_ref[pl.ds(r0, keys), hh * HEAD_DIM:(hh + 1) * HEAD_DIM]
            vc = vx_ref[pl.ds(r0, keys), hh * HEAD_DIM:(hh + 1) * HEAD_DIM]
            qs = jnp.concatenate(
                [q_ref[pl.ds(r0, BLOCK), (hh * Q_PER_KV + g) * HEAD_DIM:(hh * Q_PER_KV + g + 1) * HEAD_DIM]
                 for g in range(Q_PER_KV)], axis=0)
            s = lax.dot_general(qs, kc, (((1,), (1,)), ((), ())), preferred_element_type=F32)
            probs, inv = [], []
            for g in range(Q_PER_KV):
                sk = sink_ref[hh * Q_PER_KV + g] * LOG2_E
                lg = s[g * BLOCK:(g + 1) * BLOCK, :] * LOGIT_SCALE_LOG2 + bias
                m = jnp.maximum(jnp.max(lg, axis=-1, keepdims=True), sk)
                p = jnp.exp2(lg - m)
                inv.append(1.0 / (jnp.sum(p, axis=-1, keepdims=True) + jnp.exp2(sk - m)))
                probs.append(p.astype(BF16))
            pv = jnp.dot(jnp.concatenate(probs, axis=0), vc, preferred_element_type=F32)
            for g in range(Q_PER_KV):
                c0 = (hh * Q_PER_KV + g) * HEAD_DIM
                o_ref[pl.ds(r0, BLOCK), c0:c0 + HEAD_DIM] = (
                    pv[g * BLOCK:(g + 1) * BLOCK, :] * inv[g]).astype(o_ref.dtype)
        return carry

    lax.fori_loop(0, n_sub, sub_block, 0, unroll=2)


def _windowed_attention(qkv, sink_l, batch, seq):
    qkv3 = qkv.reshape(batch, seq, QKV_WIDTH)
    n_sub = ATT_QB // BLOCK
    n_blk = seq // BLOCK
    k_col = ATTN_WIDTH // KV_WIDTH
    v_col = k_col + 1

    def halo(col):
        prev = pl.BlockSpec((None, BLOCK, KV_WIDTH), lambda b, i: (b, jnp.maximum(i * n_sub - 1, 0), col))
        main = pl.BlockSpec((None, ATT_QB, KV_WIDTH), lambda b, i: (b, i, col))
        nxt = pl.BlockSpec((None, BLOCK, KV_WIDTH), lambda b, i: (b, jnp.minimum((i + 1) * n_sub, n_blk - 1), col))
        return [prev, main, nxt]

    out = pl.pallas_call(
        _attn_kernel,
        grid=(batch, seq // ATT_QB),
        in_specs=[pl.BlockSpec(memory_space=pltpu.SMEM),
                  pl.BlockSpec((None, ATT_QB, ATTN_WIDTH), lambda b, i: (b, i, 0))] + halo(k_col) + halo(v_col),
        out_specs=pl.BlockSpec((None, ATT_QB, ATTN_WIDTH), lambda b, i: (b, i, 0)),
        out_shape=jax.ShapeDtypeStruct((batch, seq, ATTN_WIDTH), BF16),
        scratch_shapes=[pltpu.VMEM((ATT_QB + 2 * BLOCK, KV_WIDTH), BF16),
                        pltpu.VMEM((ATT_QB + 2 * BLOCK, KV_WIDTH), BF16),
                        pltpu.VMEM((BLOCK, 3 * BLOCK), F32)],
        compiler_params=_params("parallel", "arbitrary"),
        name="window_attn",
    )(sink_l, qkv3, qkv3, qkv3, qkv3, qkv3, qkv3, qkv3)
    return out.reshape(batch * seq, ATTN_WIDTH)


def _conv_pool_kernel(yp_ref, ym_ref, yn_ref, pp_ref, pm_ref, pn_ref, wdw_ref, bdw_ref, lng_ref, lnb_ref,
                      band_ref, pw_ref, ps_ref, co_ref, po_ref, yx_ref, px_ref, sh_ref, cb_ref):
    i = pl.program_id(1)
    n_i = pl.num_programs(1)
    ts = ym_ref.shape[0]
    seq = n_i * ts

    has_prev = i > 0
    has_next = i < n_i - 1
    yx_ref[0:HALO, :] = jnp.where(has_prev, yp_ref[...], 0).astype(F32)
    yx_ref[HALO:HALO + ts, :] = ym_ref[...].astype(F32)
    yx_ref[HALO + ts:, :] = jnp.where(has_next, yn_ref[...], 0).astype(F32)
    px_ref[0:HALO, :] = jnp.where(has_prev, pp_ref[...], 0)
    px_ref[HALO:HALO + ts, :] = pm_ref[...]
    px_ref[HALO + ts:, :] = jnp.where(has_next, pn_ref[...], 0)

    n_sh = sh_ref.shape[1]
    for b in range(1, SUBLANES):
        sh_ref[b - 1] = yx_ref[b:b + n_sh, :]

    for r in range(0, ts, CP_ROWS):
        for c in range(0, CONV_WIDTH, LANES):
            acc = jnp.zeros((CP_ROWS, LANES), F32)
            for k in range(CONV_SIZE):
                a, b = divmod(k + HALO - CONV_PAD, SUBLANES)
                src = yx_ref if b == 0 else sh_ref.at[b - 1]
                r_src = r + a * SUBLANES
                acc = acc + src[r_src:r_src + CP_ROWS, c:c + LANES] * wdw_ref[k:k + 1, c:c + LANES]
            cb_ref[r:r + CP_ROWS, c:c + LANES] = acc + bdw_ref[:, c:c + LANES]

    x = cb_ref[...]
    mu = jnp.sum(x, axis=-1, keepdims=True) * (1.0 / CONV_WIDTH)
    xc = x - mu
    var = jnp.sum(xc * xc, axis=-1, keepdims=True) * (1.0 / CONV_WIDTH)
    yn = xc * lax.rsqrt(var + LN_EPS) * lng_ref[...] + lnb_ref[...]
    co_ref[...] = (yn * jax.nn.sigmoid(yn)).astype(co_ref.dtype)

    pos = i * ts + lax.broadcasted_iota(jnp.int32, (ts, 1), 0)
    for gi, size in enumerate(POOL_SIZES):
        half = size // 2
        c0 = gi * POOL_GROUP
        tot = jnp.dot(band_ref[gi], px_ref[:, c0:c0 + POOL_GROUP], preferred_element_type=F32)
        lo = jnp.clip(pos - half, 0, seq - 1)
        hi = jnp.clip(pos + half - 1, 0, seq - 1)
        cnt = (hi - lo + 1).astype(F32)
        mixed = (tot / cnt - pm_ref[:, c0:c0 + POOL_GROUP].astype(F32)).astype(BF16)
        yg = jnp.dot(mixed, pw_ref[gi], preferred_element_type=F32)
        po_ref[:, c0:c0 + POOL_GROUP] = (yg * ps_ref[:, c0:c0 + POOL_GROUP]).astype(po_ref.dtype)


def _pool_bands(ts):
    t = jnp.arange(ts)[:, None] + HALO
    r = jnp.arange(ts + 2 * HALO)[None, :]
    return jnp.stack([((r >= t - size // 2) & (r <= t + size // 2 - 1)).astype(BF16) for size in POOL_SIZES])


def _conv_pool(y, p, w_dw, b_dw, ln_g, ln_b, pool_w, pool_scale, l, batch, seq):
    y3 = y.reshape(batch, seq, CONV_WIDTH)
    p3 = p.reshape(batch, seq, POOL_WIDTH)
    per = CP_TS // HALO
    n_halo = seq // HALO
    ext = CP_TS + 2 * HALO

    def halo(width):
        prev = pl.BlockSpec((None, HALO, width), lambda b, i: (b, jnp.maximum(i * per - 1, 0), 0))
        main = pl.BlockSpec((None, CP_TS, width), lambda b, i: (b, i, 0))
        nxt = pl.BlockSpec((None, HALO, width), lambda b, i: (b, jnp.minimum((i + 1) * per, n_halo - 1), 0))
        return [prev, main, nxt]

    def layer(*shape):
        return pl.BlockSpec((None,) + shape, lambda b, i: (l,) + (0,) * len(shape))

    conv, pool = pl.pallas_call(
        _conv_pool_kernel,
        grid=(batch, seq // CP_TS),
        in_specs=halo(CONV_WIDTH) + halo(POOL_WIDTH) + [
            layer(CONV_SIZE, CONV_WIDTH), layer(1, CONV_WIDTH), layer(1, CONV_WIDTH), layer(1, CONV_WIDTH),
            pl.BlockSpec((N_POOL_GROUPS, CP_TS, ext), lambda b, i: (0, 0, 0)),
            layer(N_POOL_GROUPS, POOL_GROUP, POOL_GROUP), layer(1, POOL_WIDTH)],
        out_specs=[pl.BlockSpec((None, CP_TS, CONV_WIDTH), lambda b, i: (b, i, 0)),
                   pl.BlockSpec((None, CP_TS, POOL_WIDTH), lambda b, i: (b, i, 0))],
        out_shape=[jax.ShapeDtypeStruct((batch, seq, CONV_WIDTH), BF16),
                   jax.ShapeDtypeStruct((batch, seq, POOL_WIDTH), BF16)],
        scratch_shapes=[pltpu.VMEM((ext, CONV_WIDTH), F32),
                        pltpu.VMEM((ext, POOL_WIDTH), BF16),
                        pltpu.VMEM((SUBLANES - 1, ext - SUBLANES, CONV_WIDTH), F32),
                        pltpu.VMEM((CP_TS, CONV_WIDTH), F32)],
        compiler_params=_params("parallel", "arbitrary"),
        name="conv_pool",
    )(y3, y3, y3, p3, p3, p3, w_dw, b_dw[:, None, :], ln_g[:, None, :], ln_b[:, None, :],
      _pool_bands(CP_TS), pool_w, pool_scale[:, None, :])
    return conv.reshape(batch * seq, CONV_WIDTH), pool.reshape(batch * seq, POOL_WIDTH)


def _gated_merge_kernel(u_ref, at_ref, cv_ref, pl_ref, wg0_ref, wg1_ref, wg2_ref, wa_ref, wc_ref, wp_ref, *rest):
    n_cast = (len(rest) - 1) // 2
    o_ref = rest[n_cast]
    u = u_ref[...]

    def gate(w_ref):
        return jax.nn.sigmoid(jnp.dot(u, w_ref[...], preferred_element_type=F32))

    m = gate(wg0_ref) * jnp.dot(at_ref[...], wa_ref[...], preferred_element_type=F32)
    m = m + gate(wg1_ref) * jnp.dot(cv_ref[...], wc_ref[...], preferred_element_type=F32)
    m = m + gate(wg2_ref) * jnp.dot(pl_ref[...], wp_ref[...], preferred_element_type=F32)
    o_ref[...] = m.astype(o_ref.dtype)
    for src_ref, dst_ref in zip(rest[:n_cast], rest[n_cast + 1:]):
        dst_ref[...] = src_ref[...].astype(dst_ref.dtype)


def _gated_merge(u, attn, conv, pool, w_in, w_attn, w_conv, w_pool, casts=()):
    t, d = u.shape
    n_n = d // MERGE_TN
    row = lambda width: pl.BlockSpec((MERGE_TM, width), lambda i, n: (i, 0))
    c_in, c_args, c_out, c_shape = _cast_specs(casts, (t // MERGE_TM) * n_n, lambda i, n: i * n_n + n)
    return pl.pallas_call(
        _gated_merge_kernel,
        grid=(t // MERGE_TM, n_n),
        in_specs=[row(d), row(ATTN_WIDTH), row(CONV_WIDTH), row(POOL_WIDTH)]
        + [_cols(d, MERGE_TN, COL_GATES + b * d) for b in range(N_BRANCHES)]
        + [_cols(ATTN_WIDTH, MERGE_TN), _cols(CONV_WIDTH, MERGE_TN), _cols(POOL_WIDTH, MERGE_TN)] + c_in,
        out_specs=[pl.BlockSpec((MERGE_TM, MERGE_TN), lambda i, n: (i, n))] + c_out,
        out_shape=[jax.ShapeDtypeStruct((t, d), BF16)] + c_shape,
        compiler_params=_params("parallel", "arbitrary"),
        name="gated_merge",
    )(u, attn, conv, pool, w_in, w_in, w_in, w_attn, w_conv, w_pool, *c_args)


def _rope_lane_tables(seq):
    pos = jnp.arange(seq, dtype=F32)
    inv_freq = ROPE_THETA ** (-jnp.arange(0, ROT_DIM, 2, dtype=F32) / ROT_DIM)
    ang = pos[:, None] * inv_freq[None, :]
    cos, sin = jnp.cos(ang), jnp.sin(ang)
    ones = jnp.ones((seq, HEAD_DIM // 2 - ROT_HALF), F32)
    zeros = jnp.zeros((seq, HEAD_DIM // 2 - ROT_HALF), F32)
    cos_t = jnp.concatenate([cos, ones, cos, ones], axis=-1)
    sin_t = jnp.concatenate([-sin, zeros, sin, zeros], axis=-1)
    return cos_t, sin_t


def kernel(x, ffn1_pre_g, ffn1_post_g, ffn1_w_gate, ffn1_w_up, ffn1_w_down, mix_pre_g, mix_post_g, w_in, sink,
           w_attn_proj, conv_dw, conv_dw_b, conv_ln_g, conv_ln_b, w_conv_proj, pool_w, pool_scale, w_pool_proj,
           w_out, ffn2_pre_g, ffn2_post_g, ffn2_w_gate, ffn2_w_up, ffn2_w_down):
    batch, seq, d = x.shape
    cos_t, sin_t = _rope_lane_tables(seq)
    pool_w_b = pool_w.astype(BF16)

    h = x.reshape(batch * seq, d)
    xn, w_gate_b, w_up_b = _rmsnorm_bf16(h, ffn1_pre_g[0], casts=((ffn1_w_gate, 0), (ffn1_w_up, 0)))
    for l in range(DEPTH):
        act, w_down_b = _gate_up(xn, w_gate_b, w_up_b, ffn1_w_down, l)
        h, u, w_in_b, w_attn_b, w_conv_b, w_pool_b, w_out_b, w_qkv_b = _proj_residual(
            act, w_down_b, h, ffn1_post_g[l], mix_pre_g[l], 0.5, DOWN_TM,
            casts=((w_in, l), (w_attn_proj, l), (w_conv_proj, l), (w_pool_proj, l), (w_out, l)), emit_qkv=True)
        qkv = _qkv_proj(u, w_qkv_b, cos_t, sin_t, seq)
        y, p = _glu_pool_proj(u, w_in_b)
        attn = _windowed_attention(qkv, sink[l], batch, seq)
        conv, pool = _conv_pool(y, p, conv_dw, conv_dw_b, conv_ln_g, conv_ln_b, pool_w_b, pool_scale, l, batch, seq)
        merged, w_gate_b, w_up_b = _gated_merge(u, attn, conv, pool, w_in_b, w_attn_b, w_conv_b, w_pool_b,
                                                casts=((ffn2_w_gate, l), (ffn2_w_up, l)))
        h, xn = _proj_residual(merged, w_out_b, h, mix_post_g[l], ffn2_pre_g[l], 1.0, OUT_TM)
        act, w_down_b = _gate_up(xn, w_gate_b, w_up_b, ffn2_w_down, l)
        if l + 1 < DEPTH:
            h, xn, w_gate_b, w_up_b = _proj_residual(act, w_down_b, h, ffn2_post_g[l], ffn1_pre_g[l + 1], 0.5, DOWN_TM,
                                                     casts=((ffn1_w_gate, l + 1), (ffn1_w_up, l + 1)))
        else:
            h, = _proj_residual(act, w_down_b, h, ffn2_post_g[l], None, 0.5, DOWN_TM)
    return h.reshape(batch, seq, d)
```
